```python
import math
import jax
import jax.numpy as jnp
from jax import lax
import numpy as np

D_MODEL = 1024
BATCH = 8
SEQ = 2048
DEPTH = 4

GRID_W = 64
CTX_LEN = 256
N_EVEN = (DEPTH + 1) // 2
N_ODD = DEPTH // 2
NORM_EPS = 1e-6
ROPE_THETA = 10000.0
N_MOD = 6

ML_HEADS = 4
ML_DQK = 64
ML_DV = 128
ML_CHUNK = 64
ML_CONV = 3
ML_W = ML_HEADS * ML_DV

DA_HEADS = 4
DA_DQK = 64
DA_DV = 128
DA_W = DA_HEADS * DA_DV
Q_BLOCK = 128

EVEN_SIZES = (2 * ML_HEADS * ML_DQK, ML_W, ML_W, 4 * ML_HEADS,
              DA_HEADS * 2 * DA_DQK, DA_HEADS * 2 * DA_DQK, DA_W)
EVEN_IN = sum(EVEN_SIZES)
EVEN_SPLITS = tuple(sum(EVEN_SIZES[:i + 1]) for i in range(len(EVEN_SIZES) - 1))
MIX_W = ML_W + DA_W

WA_HEADS = 16
WA_KV_HEADS = 4
WA_GROUP = WA_HEADS // WA_KV_HEADS
WA_DH = 64
WINDOW = 128
W_BLOCK = WINDOW
WA_W = WA_HEADS * WA_DH
WA_KV_W = WA_KV_HEADS * WA_DH
ODD_IN = WA_W + 2 * WA_KV_W
ODD_SPLITS = (WA_W, WA_W + WA_KV_W)

PEER_HEADS = 8
PEER_NKEYS = 128
PEER_EXPERTS = PEER_NKEYS * PEER_NKEYS
PEER_DQ = 256
PEER_DHALF = PEER_DQ // 2
PEER_TOPK = 16
PEER_TOK_BLOCK = 128

kernel_name = 'hybrid_mlstm_diffattn_swa_peer_dit'


def _rmsnorm(x, g):
    xf = x.astype(jnp.float32)
    y = xf * lax.rsqrt(jnp.mean(xf * xf, axis=-1, keepdims=True) + NORM_EPS)
    return (y * g.astype(jnp.float32)).astype(x.dtype)


def _headnorm(x, g):
    xf = x.astype(jnp.float32)
    y = xf * lax.rsqrt(jnp.mean(xf * xf, axis=-1, keepdims=True) + NORM_EPS)
    y = y * g.astype(jnp.float32).reshape(x.shape[-2:])
    return y.reshape(x.shape[:-2] + (-1,))


def _axial_rope_tables(n_tok, dim):
    rows = n_tok // GRID_W
    r = jnp.repeat(jnp.arange(rows, dtype=jnp.float32), GRID_W)
    col = jnp.broadcast_to(jnp.arange(GRID_W, dtype=jnp.float32), (rows, GRID_W)).reshape(-1)
    nf = dim // 4
    inv = ROPE_THETA ** (-jnp.arange(nf, dtype=jnp.float32) / nf)
    ang = jnp.concatenate([r[:, None] * inv, col[:, None] * inv], axis=-1)
    return jnp.cos(ang), jnp.sin(ang)


def _apply_rope(x, cos, sin):
    shp = (x.shape[1],) + (1,) * (x.ndim - 3) + (cos.shape[-1],)
    cs = cos.reshape(shp)
    sn = sin.reshape(shp)
    xf = x.astype(jnp.float32)
    x1, x2 = jnp.split(xf, 2, axis=-1)
    return jnp.concatenate([x1 * cs - x2 * sn, x1 * sn + x2 * cs], axis=-1).astype(x.dtype)


def _centred_conv(x, w, b):
    taps = w.shape[0]
    p = taps // 2
    t = x.shape[1]
    xp = jnp.pad(x, ((0, 0), (p, p), (0, 0)))
    out = b
    for j in range(taps):
        out = out + xp[:, j:j + t] * w[j]
    return out


def _mlstm_prepare(qk, v, g, conv_w, conv_b, gate_b):
    bsz, t, _ = qk.shape
    qk = jax.nn.silu(_centred_conv(qk, conv_w, conv_b))
    q, k = jnp.split(qk, 2, axis=-1)

    def heads(a, d):
        return a.reshape(bsz, t, ML_HEADS, d).transpose(0, 2, 1, 3).astype(jnp.float32)

    q = heads(q, ML_DQK)
    k = heads(k, ML_DQK) * (ML_DQK ** -0.5)
    v = heads(v, ML_DV)
    g = (g + gate_b).astype(jnp.float32).reshape(bsz, t, 4, ML_HEADS).transpose(2, 0, 3, 1)
    return (q, k, v, g[0], jax.nn.log_sigmoid(g[1]), g[2], jax.nn.log_sigmoid(g[3]))


def _mlstm_state0(bsz):
    return (jnp.zeros((bsz, ML_HEADS, ML_DV, ML_DQK), jnp.float32),
            jnp.zeros((bsz, ML_HEADS, ML_DQK), jnp.float32),
            jnp.zeros((bsz, ML_HEADS), jnp.float32))


def _mlstm_chunk_scan(q, k, v, ig, lf, state):
    bsz, nh, t, _ = q.shape
    nc = t // ML_CHUNK

    def to_chunks(a):
        return jnp.moveaxis(a.reshape((bsz, nh, nc, ML_CHUNK) + a.shape[3:]), 2, 0)

    tri = jnp.tril(jnp.ones((ML_CHUNK, ML_CHUNK), dtype=bool))

    def step(carry, xs):
        c_st, n_st, m_st = carry
        qc, kc, vc, ic, fc = xs
        b = jnp.cumsum(fc, axis=-1)
        log_d = jnp.where(tri, b[..., :, None] - b[..., None, :] + ic[..., None, :], -jnp.inf)
        inter = b + m_st[..., None]
        m_t = jnp.maximum(inter, jnp.max(log_d, axis=-1))
        s = jnp.einsum('bhtd,bhsd->bhts', qc, kc) * jnp.exp(log_d - m_t[..., None])
        w_inter = jnp.exp(inter - m_t)
        num = (jnp.einsum('bhts,bhsv->bhtv', s, vc)
               + w_inter[..., None] * jnp.einsum('bhvd,bhtd->bhtv', c_st, qc))
        den = jnp.sum(s, axis=-1) + w_inter * jnp.einsum('bhd,bhtd->bht', n_st, qc)
        h = num / jnp.maximum(jnp.abs(den), jnp.exp(-m_t))[..., None]
        b_last = b[..., -1]
        log_w = b_last[..., None] - b + ic
        m_new = jnp.maximum(b_last + m_st, jnp.max(log_w, axis=-1))
        wk = jnp.exp(log_w - m_new[..., None])
        decay = jnp.exp(b_last + m_st - m_new)
        c_new = decay[..., None, None] * c_st + jnp.einsum('bhs,bhsv,bhsd->bhvd', wk, vc, kc)
        n_new = decay[..., None] * n_st + jnp.einsum('bhs,bhsd->bhd', wk, kc)
        return (c_new, n_new, m_new), h

    state, hs = lax.scan(step, state, tuple(to_chunks(a) for a in (q, k, v, ig, lf)))
    h = jnp.moveaxis(hs, 0, 2).reshape(bsz, nh, t, -1)
    return h, state


def _mlstm_bidir(lat, ctx_in):
    ql, kl, vl, il_f, fl_f, il_b, fl_b = lat
    qc, kc, vc, ic_f, fc_f, ic_b, fc_b = ctx_in
    bsz = ql.shape[0]

    def rev(a):
        return jnp.flip(a, axis=2)

    hc_f, st_f = _mlstm_chunk_scan(qc, kc, vc, ic_f, fc_f, _mlstm_state0(bsz))
    hl_f, _ = _mlstm_chunk_scan(ql, kl, vl, il_f, fl_f, st_f)
    hc_b, st_b = _mlstm_chunk_scan(rev(qc), rev(kc), rev(vc), rev(ic_b), rev(fc_b), _mlstm_state0(bsz))
    hl_b, _ = _mlstm_chunk_scan(rev(ql), rev(kl), rev(vl), rev(il_b), rev(fl_b), st_b)
    return hl_f + rev(hl_b), hc_f + rev(hc_b)


def _mlstm_out(h, o, g, dtype):
    y = _headnorm(h.transpose(0, 2, 1, 3), g)
    return (y * jax.nn.sigmoid(o.astype(jnp.float32))).astype(dtype)


def _da_heads(q, k, v):
    bsz, t, _ = q.shape
    return (q.reshape(bsz, t, DA_HEADS, 2, DA_DQK),
            k.reshape(bsz, t, DA_HEADS, 2, DA_DQK),
            v.reshape(bsz, t, DA_HEADS, DA_DV))


def _diff_lambda(lp, lam_init):
    lp = lp.astype(jnp.float32)
    return jnp.exp(jnp.sum(lp[0] * lp[1])) - jnp.exp(jnp.sum(lp[2] * lp[3])) + lam_init


def _diff_attend(q, k, v, lam):
    s = jnp.einsum('bqhmd,bkhmd->bhmqk', q, k).astype(jnp.float32) * (DA_DQK ** -0.5)
    p = jax.nn.softmax(s, axis=-1)
    p = p[:, :, 0] - lam * p[:, :, 1]
    return jnp.einsum('bhqk,bkhv->bqhv', p.astype(v.dtype), v)


def _blocked_queries(fn, q):
    bsz, s = q.shape[:2]
    nb = s // Q_BLOCK
    qb = jnp.moveaxis(q.reshape((bsz, nb, Q_BLOCK) + q.shape[2:]), 1, 0)
    out = jnp.moveaxis(lax.map(fn, qb), 0, 1)
    return out.reshape((bsz, s) + out.shape[3:])


def _da_out(o, g, lam_init, dtype):
    return (_headnorm(o, g) * (1.0 - lam_init)).astype(dtype)


def _even_mixer(xl, xc, w_in, conv_w, conv_b, gate_b, ml_g, da_lam, da_g, w_out,
                lam_init, cos, sin, need_ctx):
    dt = xl.dtype
    pl = jnp.split(xl @ w_in, EVEN_SPLITS, axis=-1)
    pc = jnp.split(xc @ w_in, EVEN_SPLITS, axis=-1)
    lat = _mlstm_prepare(pl[0], pl[1], pl[3], conv_w, conv_b, gate_b)
    cin = _mlstm_prepare(pc[0], pc[1], pc[3], conv_w, conv_b, gate_b)
    hl, hc = _mlstm_bidir(lat, cin)
    ml_l = _mlstm_out(hl, pl[2], ml_g, dt)
    lam = _diff_lambda(da_lam, lam_init)
    ql, kl, vl = _da_heads(pl[4], pl[5], pl[6])
    qc, kc, vc = _da_heads(pc[4], pc[5], pc[6])
    ql = _apply_rope(ql, cos, sin)
    kl = _apply_rope(kl, cos, sin)
    kk = jnp.concatenate([kc, kl], axis=1)
    vv = jnp.concatenate([vc, vl], axis=1)
    da_l = _blocked_queries(lambda qb: _diff_attend(qb, kk, vv, lam), ql)
    yl = jnp.concatenate([ml_l, _da_out(da_l, da_g, lam_init, dt)], axis=-1) @ w_out
    yc = None
    if need_ctx:
        ml_c = _mlstm_out(hc, pc[2], ml_g, dt)
        da_c = _da_out(_diff_attend(qc, kc, vc, lam), da_g, lam_init, dt)
        yc = jnp.concatenate([ml_c, da_c], axis=-1) @ w_out
    return yl, yc


def _sink_softmax(s, sink):
    sk = jnp.broadcast_to(sink[None, :, :, None, None], s.shape[:-1] + (1,))
    p = jax.nn.softmax(jnp.concatenate([s, sk], axis=-1), axis=-1)
    return p[..., :-1]


def _ctx_gqa_attend(q, kc, vc, sink):
    bsz, t = q.shape[:2]
    qg = q.reshape(bsz, t, WA_KV_HEADS, WA_GROUP, WA_DH)
    s = jnp.einsum('bqhgd,bkhd->bhgqk', qg, kc).astype(jnp.float32) * (WA_DH ** -0.5)
    p = _sink_softmax(s, sink)
    o = jnp.einsum('bhgqk,bkhd->bqhgd', p.astype(vc.dtype), vc)
    return o.reshape(bsz, t, WA_W)


def _window_attend(q, k, v, kc, vc, sink):
    bsz, s = q.shape[:2]
    nb = s // W_BLOCK
    tc = kc.shape[1]
    qb = q.reshape(bsz, nb, W_BLOCK, WA_KV_HEADS, WA_GROUP, WA_DH)
    pad = ((0, 0), (W_BLOCK, W_BLOCK), (0, 0), (0, 0))
    kp = jnp.pad(k, pad).reshape(bsz, nb + 2, W_BLOCK, WA_KV_HEADS, WA_DH)
    vp = jnp.pad(v, pad).reshape(bsz, nb + 2, W_BLOCK, WA_KV_HEADS, WA_DH)
    kw = jnp.concatenate([kp[:, :-2], kp[:, 1:-1], kp[:, 2:]], axis=2)
    vw = jnp.concatenate([vp[:, :-2], vp[:, 1:-1], vp[:, 2:]], axis=2)
    qpos = jnp.arange(nb)[:, None] * W_BLOCK + jnp.arange(W_BLOCK)[None, :]
    kpos = (jnp.arange(nb)[:, None] - 1) * W_BLOCK + jnp.arange(3 * W_BLOCK)[None, :]
    mask = ((jnp.abs(qpos[:, :, None] - kpos[:, None, :]) <= WINDOW)
            & (kpos[:, None, :] >= 0) & (kpos[:, None, :] < s))
    scale = WA_DH ** -0.5

    def blk(args):
        qi, ki, vi, mi = args
        s_loc = jnp.einsum('bqhgd,bkhd->bhgqk', qi, ki).astype(jnp.float32) * scale
        s_loc = jnp.where(mi, s_loc, -jnp.inf)
        s_ctx = jnp.einsum('bqhgd,bkhd->bhgqk', qi, kc).astype(jnp.float32) * scale
        p = _sink_softmax(jnp.concatenate([s_ctx, s_loc], axis=-1), sink).astype(vi.dtype)
        return (jnp.einsum('bhgqk,bkhd->bqhgd', p[..., :tc], vc)
                + jnp.einsum('bhgqk,bkhd->bqhgd', p[..., tc:], vi))

    out = lax.map(blk, (jnp.moveaxis(qb, 1, 0), jnp.moveaxis(kw, 1, 0), jnp.moveaxis(vw, 1, 0), mask))
    return jnp.moveaxis(out, 0, 1).reshape(bsz, s, WA_W)


def _odd_mixer(xl, xc, w_in, sink, w_out, cos, sin, need_ctx):
    bsz, s, _ = xl.shape
    tc = xc.shape[1]
    ql, kl, vl = jnp.split(xl @ w_in, ODD_SPLITS, axis=-1)
    ql = _apply_rope(ql.reshape(bsz, s, WA_HEADS, WA_DH), cos, sin)
    kl = _apply_rope(kl.reshape(bsz, s, WA_KV_HEADS, WA_DH), cos, sin)
    vl = vl.reshape(bsz, s, WA_KV_HEADS, WA_DH)
    kc, vc = jnp.split(xc @ w_in[:, WA_W:], 2, axis=-1)
    kc = kc.reshape(bsz, tc, WA_KV_HEADS, WA_DH)
    vc = vc.reshape(bsz, tc, WA_KV_HEADS, WA_DH)
    sink = sink.astype(jnp.float32).reshape(WA_KV_HEADS, WA_GROUP)
    yl = _window_attend(ql, kl, vl, kc, vc, sink) @ w_out
    yc = None
    if need_ctx:
        qc = (xc @ w_in[:, :WA_W]).reshape(bsz, tc, WA_HEADS, WA_DH)
        yc = _ctx_gqa_attend(qc, kc, vc, sink) @ w_out
    return yl, yc


def _peer_ffn(x, w_q, keys, u, v):
    shp = x.shape
    xb = x.reshape(-1, PEER_TOK_BLOCK, shp[-1])

    def block(xt):
        n = xt.shape[0]
        q = (xt @ w_q).reshape(n, PEER_HEADS, 2, PEER_DHALF)
        s = jnp.einsum('nhpd,hpkd->nhpk', q, keys).astype(jnp.float32)
        sv, si = lax.top_k(s, PEER_TOPK)
        cand = sv[:, :, 0, :, None] + sv[:, :, 1, None, :]
        cidx = si[:, :, 0, :, None] * PEER_NKEYS + si[:, :, 1, None, :]
        cs, ci = lax.top_k(cand.reshape(n, PEER_HEADS, -1), PEER_TOPK)
        eidx = jnp.take_along_axis(cidx.reshape(n, PEER_HEADS, -1), ci, axis=-1)
        gate = jax.nn.softmax(cs, axis=-1)
        a = jnp.einsum('nd,nhkd->nhk', xt, u[eidx]).astype(jnp.float32)
        w = (gate * jax.nn.gelu(a, approximate=False)).astype(xt.dtype)
        return jnp.einsum('nhk,nhkd->nd', w, v[eidx])

    return lax.map(block, xb).reshape(shp)


def setup_inputs(seed: int = 0) -> dict:
    key = jax.random.key(seed)
    ks = iter(jax.random.split(key, 32))

    def nrm(shape, scale):
        return jax.random.normal(next(ks), shape, jnp.float32) * scale

    d = D_MODEL
    zeros_h = jnp.zeros((ML_HEADS,), jnp.float32)
    f_bias = jnp.linspace(3.0, 6.0, ML_HEADS, dtype=jnp.float32)
    gate_base = jnp.concatenate([zeros_h, f_bias, zeros_h, f_bias])
    return {
        'x': nrm((BATCH, SEQ, d), 1.0),
        'c': nrm((BATCH, d), 1.0),
        'ctx': nrm((BATCH, CTX_LEN, d), 1.0),
        'c_ctx': nrm((d,), 1.0),
        'mod_w': nrm((DEPTH, d, N_MOD * d), 0.5 * d ** -0.5),
        'mod_b': nrm((DEPTH, N_MOD * d), 0.02),
        'norm1_g': 1.0 + nrm((DEPTH, d), 0.02),
        'norm2_g': 1.0 + nrm((DEPTH, d), 0.02),
        'ev_w_in': nrm((N_EVEN, d, EVEN_IN), d ** -0.5),
        'ev_ml_conv_w': nrm((N_EVEN, ML_CONV, 2 * ML_HEADS * ML_DQK), ML_CONV ** -0.5),
        'ev_ml_conv_b': nrm((N_EVEN, 2 * ML_HEADS * ML_DQK), 0.02),
        'ev_ml_gate_b': gate_base[None, :] + nrm((N_EVEN, 4 * ML_HEADS), 0.1),
        'ev_ml_norm_g': 1.0 + nrm((N_EVEN, ML_W), 0.02),
        'ev_da_lam': nrm((N_EVEN, 4, DA_DQK), 0.1),
        'ev_da_norm_g': 1.0 + nrm((N_EVEN, DA_W), 0.02),
        'ev_w_out': nrm((N_EVEN, MIX_W, d), MIX_W ** -0.5),
        'od_w_in': nrm((N_ODD, d, ODD_IN), d ** -0.5),
        'od_sink': nrm((N_ODD, WA_HEADS), 1.0),
        'od_w_out': nrm((N_ODD, WA_W, d), WA_W ** -0.5),
        'pr_w_q': nrm((DEPTH, d, PEER_HEADS * PEER_DQ), d ** -0.5),
        'pr_keys': nrm((DEPTH, PEER_HEADS, 2, PEER_NKEYS, PEER_DHALF), PEER_DHALF ** -0.5),
        'pr_u': nrm((DEPTH, PEER_EXPERTS, d), d ** -0.5),
        'pr_v': nrm((DEPTH, PEER_EXPERTS, d), PEER_HEADS ** -0.5),
        'final_g': 1.0 + nrm((d,), 0.02),
    }


def reference(x, c, ctx, c_ctx, mod_w, mod_b, norm1_g, norm2_g, ev_w_in, ev_ml_conv_w,
              ev_ml_conv_b, ev_ml_gate_b, ev_ml_norm_g, ev_da_lam, ev_da_norm_g, ev_w_out,
              od_w_in, od_sink, od_w_out, pr_w_q, pr_keys, pr_u, pr_v, final_g):
    bsz, s, d = x.shape
    cos_da, sin_da = _axial_rope_tables(s, DA_DQK)
    cos_wa, sin_wa = _axial_rope_tables(s, WA_DH)
    sc = jax.nn.silu(c)
    scc = jax.nn.silu(c_ctx)
    hl, hc = x, ctx
    for i in range(DEPTH):
        need_ctx = i < DEPTH - 1
        j = i // 2
        ml = (sc @ mod_w[i] + mod_b[i]).reshape(bsz, 1, N_MOD, d)
        mc = (scc @ mod_w[i] + mod_b[i]).reshape(N_MOD, d)
        xl = _rmsnorm(hl, norm1_g[i]) * (1.0 + ml[:, :, 1]) + ml[:, :, 0]
        xc = _rmsnorm(hc, norm1_g[i]) * (1.0 + mc[1]) + mc[0]
        if i % 2 == 0:
            lam_init = 0.8 - 0.6 * math.exp(-0.3 * i)
            yl, yc = _even_mixer(xl, xc, ev_w_in[j], ev_ml_conv_w[j], ev_ml_conv_b[j],
                                 ev_ml_gate_b[j], ev_ml_norm_g[j], ev_da_lam[j], ev_da_norm_g[j],
                                 ev_w_out[j], lam_init, cos_da, sin_da, need_ctx)
        else:
            yl, yc = _odd_mixer(xl, xc, od_w_in[j], od_sink[j], od_w_out[j],
                                cos_wa, sin_wa, need_ctx)
        hl = hl + ml[:, :, 2] * yl
        xl = _rmsnorm(hl, norm2_g[i]) * (1.0 + ml[:, :, 4]) + ml[:, :, 3]
        hl = hl + ml[:, :, 5] * _peer_ffn(xl, pr_w_q[i], pr_keys[i], pr_u[i], pr_v[i])
        if need_ctx:
            hc = hc + mc[2] * yc
            xc = _rmsnorm(hc, norm2_g[i]) * (1.0 + mc[4]) + mc[3]
            hc = hc + mc[5] * _peer_ffn(xc, pr_w_q[i], pr_keys[i], pr_u[i], pr_v[i])
    return _rmsnorm(hl, final_g)
```

```python
import functools
import math

import jax
import jax.numpy as jnp
from jax import lax
from jax.experimental import pallas as pl
from jax.experimental.pallas import tpu as pltpu

D = 1024
CTX = 256
GRID_W = 64
EPS = 1e-6
ROPE_THETA = 10000.0
N_MOD = 6

ML_HEADS = 4
ML_DQK = 64
ML_DV = 128
DA_HEADS = 4
DA_DQK = 64
DA_DV = 128
WA_HEADS = 16
WA_KV_HEADS = 4
WA_GROUP = 4
WA_DH = 64
WINDOW = 128

PEER_HEADS = 8
PEER_NKEYS = 128
PEER_DHALF = 128
PEER_TOPK = 16

TM = 256
TQ_ML = 256
TQ_DA = 256
TQ_WA = 128
PEER_TN = 256
PEER_TE = 1024
VMEM_LIMIT = 56 * 1024 * 1024

BF = jnp.bfloat16
F32 = jnp.float32
NEG_INF = float("-inf")


def _cparams(sem):
    return pltpu.CompilerParams(dimension_semantics=sem, vmem_limit_bytes=VMEM_LIMIT)


def _nt_dot(a, b):
    return lax.dot_general(a, b, (((1,), (1,)), ((), ())), preferred_element_type=F32)


def _dot(a, b):
    return jnp.dot(a, b, preferred_element_type=F32)


def _mod_kernel(cc_ref, w_ref, b_ref, o_ref):
    a = jax.nn.silu(cc_ref[...]).astype(BF)
    o_ref[0] = _dot(a, w_ref[0].astype(BF)) + b_ref[0]


def _modulation(cc, mod_w, mod_b):
    depth = mod_w.shape[0]
    rows = cc.shape[0]
    tn = 1536
    return pl.pallas_call(
        _mod_kernel,
        grid=(depth, N_MOD * D // tn),
        in_specs=[pl.BlockSpec((rows, D), lambda l, n: (0, 0)),
                  pl.BlockSpec((1, D, tn), lambda l, n: (l, 0, n)),
                  pl.BlockSpec((1, 1, tn), lambda l, n: (l, 0, n))],
        out_specs=pl.BlockSpec((1, rows, tn), lambda l, n: (l, 0, n)),
        out_shape=jax.ShapeDtypeStruct((depth, rows, N_MOD * D), F32),
        compiler_params=_cparams(("parallel", "parallel")),
    )(cc, mod_w, mod_b.reshape(depth, 1, N_MOD * D))


def _mod_spec():
    return pl.BlockSpec((1, 1, N_MOD, D), lambda b, t: (b, jnp.minimum(t // (CTX // TM), 1), 0, 0))


def _proj_kernel(*refs, shift_idx, scale_idx, nrope, want_xn):
    it = iter(refs)
    h_ref, mod_ref, g_ref, wm_ref = next(it), next(it), next(it), next(it)
    if nrope:
        wr_ref, cos_ref, sin_ref = next(it), next(it), next(it)
    y_ref = next(it)
    x = h_ref[0]
    xn = x * lax.rsqrt(jnp.mean(x * x, axis=-1, keepdims=True) + EPS) * g_ref[...]
    mod = mod_ref[0, 0]
    xm = xn * (1.0 + mod[scale_idx:scale_idx + 1]) + mod[shift_idx:shift_idx + 1]
    xb = xm.astype(BF)
    acc = _dot(xb, wm_ref[...])
    if nrope:
        rot = _dot(xb, wr_ref[...])
        y_ref[0, :, :nrope] = acc[:, :nrope] * cos_ref[...] + rot * sin_ref[...]
        y_ref[0, :, nrope:] = acc[:, nrope:]
    else:
        y_ref[0] = acc
    if want_xn:
        next(it)[0] = xb


def _norm_mod_proj(h, modtab, g, wm, wr=None, cos=None, sin=None, *, shift_idx, scale_idx, want_xn=False):
    bsz, t, _ = h.shape
    nm = wm.shape[1]
    nrope = 0 if wr is None else wr.shape[1]
    in_specs = [pl.BlockSpec((1, TM, D), lambda b, i: (b, i, 0)),
                _mod_spec(),
                pl.BlockSpec((1, D), lambda b, i: (0, 0)),
                pl.BlockSpec((D, nm), lambda b, i: (0, 0))]
    args = [h, modtab, g.reshape(1, D), wm]
    if nrope:
        in_specs += [pl.BlockSpec((D, nrope), lambda b, i: (0, 0)),
                     pl.BlockSpec((TM, nrope), lambda b, i: (i, 0)),
                     pl.BlockSpec((TM, nrope), lambda b, i: (i, 0))]
        args += [wr, cos, sin]
    out_specs = [pl.BlockSpec((1, TM, nm), lambda b, i: (b, i, 0))]
    out_shape = [jax.ShapeDtypeStruct((bsz, t, nm), F32)]
    if want_xn:
        out_specs.append(pl.BlockSpec((1, TM, D), lambda b, i: (b, i, 0)))
        out_shape.append(jax.ShapeDtypeStruct((bsz, t, D), BF))
    return pl.pallas_call(
        functools.partial(_proj_kernel, shift_idx=shift_idx, scale_idx=scale_idx, nrope=nrope, want_xn=want_xn),
        grid=(bsz, t // TM), in_specs=in_specs, out_specs=out_specs, out_shape=out_shape,
        compiler_params=_cparams(("parallel", "parallel")),
    )(*args)


def _outproj_kernel(*refs, gate_idx, n_src):
    h_ref, mod_ref = refs[0], refs[1]
    o_ref = refs[2 + 2 * n_src]
    acc = None
    for s in range(n_src):
        part = _dot(refs[2 + 2 * s][0].astype(BF), refs[3 + 2 * s][...])
        acc = part if acc is None else acc + part
    gate = mod_ref[0, 0][gate_idx:gate_idx + 1]
    o_ref[0] = h_ref[0] + gate * acc


def _outproj_residual(h, modtab, srcs, *, gate_idx):
    bsz, t, _ = h.shape
    in_specs = [pl.BlockSpec((1, TM, D), lambda b, i: (b, i, 0)), _mod_spec()]
    args = [h, modtab]
    for x, w in srcs:
        k = x.shape[-1]
        in_specs += [pl.BlockSpec((1, TM, k), lambda b, i: (b, i, 0)),
                     pl.BlockSpec((k, D), lambda b, i: (0, 0))]
        args += [x, w]
    return pl.pallas_call(
        functools.partial(_outproj_kernel, gate_idx=gate_idx, n_src=len(srcs)),
        grid=(bsz, t // TM), in_specs=in_specs,
        out_specs=pl.BlockSpec((1, TM, D), lambda b, i: (b, i, 0)),
        out_shape=jax.ShapeDtypeStruct((bsz, t, D), F32),
        compiler_params=_cparams(("parallel", "parallel")),
    )(*args)


def _lane_scan(x, op, ident, lane, reverse):
    for k in range(7):
        s = 1 << k
        if reverse:
            sh = pltpu.roll(x, 128 - s, 1)
            x = op(x, jnp.where(lane < 128 - s, sh, ident))
        else:
            sh = pltpu.roll(x, s, 1)
            x = op(x, jnp.where(lane >= s, sh, ident))
    return x


def _gate_scan_kernel(g_ref, b_ref, gk_ref, qs_ref, *, t):
    nb = t // 128
    nh = ML_HEADS
    lane = lax.broadcasted_iota(jnp.int32, (nh, 128), 1)
    pre = g_ref[0] + b_ref[...]
    for d in range(2):
        ig = pre[(2 * d) * nh:(2 * d + 1) * nh]
        lf = jax.nn.log_sigmoid(pre[(2 * d + 1) * nh:(2 * d + 2) * nh])
        if d == 0:
            order = list(range(nb))
        else:
            order = list(range(CTX // 128 - 1, -1, -1)) + list(range(nb - 1, CTX // 128 - 1, -1))
        carry_b = jnp.zeros((nh, 1), F32)
        carry_m = jnp.full((nh, 1), NEG_INF, F32)
        edge = 0 if d else 127
        for blk in order:
            sl = slice(blk * 128, (blk + 1) * 128)
            bb = _lane_scan(lf[:, sl], jnp.add, 0.0, lane, bool(d)) + carry_b
            gg = ig[:, sl] - bb
            mm = jnp.maximum(_lane_scan(gg, jnp.maximum, NEG_INF, lane, bool(d)), carry_m)
            carry_b = bb[:, edge:edge + 1]
            carry_m = mm[:, edge:edge + 1]
            gk_ref[0, d * nh:(d + 1) * nh, sl] = gg
            qs_ref[0, (2 * d) * nh:(2 * d + 1) * nh, sl] = mm
            qs_ref[0, (2 * d + 1) * nh:(2 * d + 2) * nh, sl] = bb + mm


def _gate_scan(gates_t, gate_b):
    bsz, ng, t = gates_t.shape
    return pl.pallas_call(
        functools.partial(_gate_scan_kernel, t=t),
        grid=(bsz,),
        in_specs=[pl.BlockSpec((1, ng, t), lambda b: (b, 0, 0)),
                  pl.BlockSpec((ng, 1), lambda b: (0, 0))],
        out_specs=[pl.BlockSpec((1, 2 * ML_HEADS, t), lambda b: (b, 0, 0)),
                   pl.BlockSpec((1, 4 * ML_HEADS, t), lambda b: (b, 0, 0))],
        out_shape=[jax.ShapeDtypeStruct((bsz, 2 * ML_HEADS, t), F32),
                   jax.ShapeDtypeStruct((bsz, 4 * ML_HEADS, t), F32)],
        compiler_params=_cparams(("parallel",)),
    )(gates_t, gate_b.reshape(ng, 1))


def _mlstm_kernel(qk_ref, v_ref, o_ref, gk_ref, qs_ref, cw_ref, cb_ref, ng_ref, out_ref,
                  q_s, k_s, v_s, *, t):
    hd = pl.program_id(1)
    x = qk_ref[0]
    row = lax.broadcasted_iota(jnp.int32, (t, 1), 0)
    xp = jnp.where((row == 0) | (row == CTX), 0.0, pltpu.roll(x, 1, 0))
    xn = jnp.where((row == CTX - 1) | (row == t - 1), 0.0, pltpu.roll(x, t - 1, 0))
    cw = cw_ref[...]
    y = cb_ref[...] + xp * cw[0:1] + x * cw[1:2] + xn * cw[2:3]
    y = jax.nn.silu(y)
    q_s[...] = y[:, :ML_DQK].astype(BF)
    k_s[...] = (y[:, ML_DQK:] * (ML_DQK ** -0.5)).astype(BF)
    v_s[...] = v_ref[0].astype(BF)
    col = lax.broadcasted_iota(jnp.int32, (1, t), 1)
    col_b = jnp.where(col < CTX, CTX - 1 - col, t - 1 + CTX - col)

    def q_block(qi, carry):
        r0 = pl.multiple_of(qi * TQ_ML, TQ_ML)
        qb = q_s[pl.ds(r0, TQ_ML), :]
        s = _nt_dot(qb, k_s[...])
        rowa = r0 + lax.broadcasted_iota(jnp.int32, (TQ_ML, 1), 0)
        qs = qs_ref[0, 0, pl.ds(r0, TQ_ML), :]
        hsum = None
        for d in range(2):
            if d == 0:
                allowed = col <= rowa
            else:
                allowed = col_b <= jnp.where(rowa < CTX, CTX - 1 - rowa, t - 1 + CTX - rowa)
            gk = gk_ref[0, pl.ds(d * ML_HEADS + hd, 1), :]
            p = s * jnp.exp(jnp.where(allowed, gk - qs[:, 2 * d:2 * d + 1], NEG_INF))
            den = jnp.sum(p, axis=-1, keepdims=True)
            num = _dot(p.astype(BF), v_s[...])
            hdir = num / jnp.maximum(jnp.abs(den), jnp.exp(-qs[:, 2 * d + 1:2 * d + 2]))
            hsum = hdir if hsum is None else hsum + hdir
        yn = hsum * lax.rsqrt(jnp.mean(hsum * hsum, axis=-1, keepdims=True) + EPS) * ng_ref[...]
        out_ref[0, pl.ds(r0, TQ_ML), :] = yn * jax.nn.sigmoid(o_ref[0, pl.ds(r0, TQ_ML), :])
        return carry

    lax.fori_loop(0, t // TQ_ML, q_block, 0)


def _mlstm(y, gk, qs, conv_w, conv_b, norm_g, *, col_qk, col_v, col_o):
    bsz, t, _ = y.shape
    w = ML_HEADS * ML_DV
    return pl.pallas_call(
        functools.partial(_mlstm_kernel, t=t),
        grid=(bsz, ML_HEADS),
        in_specs=[pl.BlockSpec((1, t, 128), lambda b, h: (b, 0, col_qk // 128 + h)),
                  pl.BlockSpec((1, t, 128), lambda b, h: (b, 0, col_v // 128 + h)),
                  pl.BlockSpec((1, t, 128), lambda b, h: (b, 0, col_o // 128 + h)),
                  pl.BlockSpec((1, 2 * ML_HEADS, t), lambda b, h: (b, 0, 0)),
                  pl.BlockSpec((1, 1, t, 4), lambda b, h: (b, h, 0, 0)),
                  pl.BlockSpec((3, 128), lambda b, h: (0, h)),
                  pl.BlockSpec((1, 128), lambda b, h: (0, h)),
                  pl.BlockSpec((1, 128), lambda b, h: (0, h))],
        out_specs=pl.BlockSpec((1, t, 128), lambda b, h: (b, 0, h)),
        out_shape=jax.ShapeDtypeStruct((bsz, t, w), F32),
        scratch_shapes=[pltpu.VMEM((t, ML_DQK), BF), pltpu.VMEM((t, ML_DQK), BF), pltpu.VMEM((t, ML_DV), BF)],
        compiler_params=_cparams(("parallel", "parallel")),
    )(y, y, y, gk, qs, conv_w, conv_b.reshape(1, w), norm_g.reshape(1, w))


def _da_kernel(q_ref, k_ref, v_ref, lam_ref, ng_ref, out_ref, *, t, lam_init):
    qt = pl.program_id(2)
    lp = lam_ref[...]
    lam = (jnp.exp(jnp.sum(lp[0:1] * lp[1:2], axis=-1, keepdims=True))
           - jnp.exp(jnp.sum(lp[2:3] * lp[3:4], axis=-1, keepdims=True)) + lam_init)
    q = q_ref[0] * (DA_DQK ** -0.5)

    def attend(nk):
        v = v_ref[0, :nk, :].astype(BF)
        outs = []
        for m in range(2):
            qm = q[:, m * DA_DQK:(m + 1) * DA_DQK].astype(BF)
            km = k_ref[0, :nk, m * DA_DQK:(m + 1) * DA_DQK].astype(BF)
            s = _nt_dot(qm, km)
            e = jnp.exp(s - jnp.max(s, axis=-1, keepdims=True))
            outs.append(_dot(e.astype(BF), v) / jnp.sum(e, axis=-1, keepdims=True))
        o = outs[0] - lam * outs[1]
        yn = o * lax.rsqrt(jnp.mean(o * o, axis=-1, keepdims=True) + EPS) * ng_ref[...]
        out_ref[0] = yn * (1.0 - lam_init)

    @pl.when(qt < CTX // TQ_DA)
    def _():
        attend(CTX)

    @pl.when(qt >= CTX // TQ_DA)
    def _():
        attend(t)


def _diff_attention(y, da_lam, norm_g, *, col_q, col_k, col_v, lam_init):
    bsz, t, _ = y.shape
    w = DA_HEADS * DA_DV
    return pl.pallas_call(
        functools.partial(_da_kernel, t=t, lam_init=lam_init),
        grid=(bsz, DA_HEADS, t // TQ_DA),
        in_specs=[pl.BlockSpec((1, TQ_DA, 128), lambda b, h, i: (b, i, col_q // 128 + h)),
                  pl.BlockSpec((1, t, 128), lambda b, h, i: (b, 0, col_k // 128 + h)),
                  pl.BlockSpec((1, t, 128), lambda b, h, i: (b, 0, col_v // 128 + h)),
                  pl.BlockSpec((4, DA_DQK), lambda b, h, i: (0, 0)),
                  pl.BlockSpec((1, 128), lambda b, h, i: (0, h))],
        out_specs=pl.BlockSpec((1, TQ_DA, 128), lambda b, h, i: (b, i, h)),
        out_shape=jax.ShapeDtypeStruct((bsz, t, w), F32),
        compiler_params=_cparams(("parallel", "parallel", "parallel")),
    )(y, y, y, da_lam, norm_g.reshape(1, w))


def _wa_kernel(q_ref, k_ref, v_ref, sink_ref, out_ref, *, t):
    i = pl.program_id(1)
    n_ctx_blk = CTX // TQ_WA
    n_lat = t - CTX
    span = 3 * WINDOW

    def attend(local):
        if local:
            lat0 = jnp.clip((i - n_ctx_blk - 1) * TQ_WA, 0, n_lat - span)
            start = pl.multiple_of(CTX + lat0, TQ_WA)
            qpos = (i - n_ctx_blk) * TQ_WA + lax.broadcasted_iota(jnp.int32, (TQ_WA, 1), 0)
            kpos = lat0 + lax.broadcasted_iota(jnp.int32, (1, span), 1)
            mask = jnp.abs(qpos - kpos) <= WINDOW
        for g in range(WA_KV_HEADS):
            cs = slice(g * WA_DH, (g + 1) * WA_DH)
            kc = k_ref[0, :CTX, cs].astype(BF)
            vc = v_ref[0, :CTX, cs].astype(BF)
            if local:
                kl = k_ref[0, pl.ds(start, span), cs].astype(BF)
                vl = v_ref[0, pl.ds(start, span), cs].astype(BF)
            outs = []
            for hh in range(WA_GROUP):
                hd = g * WA_GROUP + hh
                q = (q_ref[0, :, hd * WA_DH:(hd + 1) * WA_DH] * (WA_DH ** -0.5)).astype(BF)
                sink = sink_ref[:, hd:hd + 1]
                sc = _nt_dot(q, kc)
                mx = jnp.maximum(jnp.max(sc, axis=-1, keepdims=True), sink)
                if local:
                    sl = jnp.where(mask, _nt_dot(q, kl), NEG_INF)
                    mx = jnp.maximum(mx, jnp.max(sl, axis=-1, keepdims=True))
                ec = jnp.exp(sc - mx)
                den = jnp.sum(ec, axis=-1, keepdims=True) + jnp.exp(sink - mx)
                num = _dot(ec.astype(BF), vc)
                if local:
                    el = jnp.exp(sl - mx)
                    den = den + jnp.sum(el, axis=-1, keepdims=True)
                    num = num + _dot(el.astype(BF), vl)
                outs.append(num / den)
            out_ref[0, :, g * WA_GROUP * WA_DH:(g + 1) * WA_GROUP * WA_DH] = jnp.concatenate(outs, axis=1)

    @pl.when(i < n_ctx_blk)
    def _():
        attend(False)

    @pl.when(i >= n_ctx_blk)
    def _():
        attend(True)


def _window_attention(y, sink, *, col_q, col_k, col_v):
    bsz, t, _ = y.shape
    qw = WA_HEADS * WA_DH
    kw = WA_KV_HEADS * WA_DH
    return pl.pallas_call(
        functools.partial(_wa_kernel, t=t),
        grid=(bsz, t // TQ_WA),
        in_specs=[pl.BlockSpec((1, TQ_WA, qw), lambda b, i: (b, i, col_q // qw)),
                  pl.BlockSpec((1, t, kw), lambda b, i: (b, 0, col_k // kw)),
                  pl.BlockSpec((1, t, kw), lambda b, i: (b, 0, col_v // kw)),
                  pl.BlockSpec((1, WA_HEADS), lambda b, i: (0, 0))],
        out_specs=pl.BlockSpec((1, TQ_WA, qw), lambda b, i: (b, i, 0)),
        out_shape=jax.ShapeDtypeStruct((bsz, t, qw), F32),
        compiler_params=_cparams(("parallel", "parallel")),
    )(y, y, y, sink.reshape(1, WA_HEADS))


def _pop_max(x, rowid):
    m = jnp.max(x, axis=0, keepdims=True)
    first = jnp.min(jnp.where(x == m, rowid, float(x.shape[0])), axis=0, keepdims=True)
    return m, jnp.where(rowid == first, NEG_INF, x)


_CAND = [(a, b) for a in range(PEER_TOPK) for b in range(PEER_TOPK) if (a + 1) * (b + 1) <= PEER_TOPK]


def _peer_kernel(q_ref, xn_ref, keys_ref, u_ref, vt_ref, h_ref, mod_ref, out_ref,
                 s0_s, s1_s, c0_s, p1_s, tau_s, w_s, acc_s, *, gate_idx):
    e = pl.program_id(1)
    tn = q_ref.shape[0]

    @pl.when(e == 0)
    def _route():
        rowid = lax.broadcasted_iota(jnp.int32, (PEER_NKEYS, tn), 0).astype(F32)
        crow = lax.broadcasted_iota(jnp.int32, (len(_CAND), tn), 0).astype(F32)
        for h in range(PEER_HEADS):
            tops = []
            for p in range(2):
                c0 = (2 * h + p) * PEER_DHALF
                qhp = q_ref[:, c0:c0 + PEER_DHALF].astype(BF)
                s = _nt_dot(keys_ref[h, p], qhp)
                (s0_s if p == 0 else s1_s)[h] = s
                vals = []
                for _ in range(PEER_TOPK):
                    m, s = _pop_max(s, rowid)
                    vals.append(m)
                tops.append(vals)
            cand = jnp.concatenate([tops[0][a] + tops[1][b] for a, b in _CAND], axis=0)
            work = cand
            for _ in range(PEER_TOPK):
                tau, work = _pop_max(work, crow)
            cmax = tops[0][0] + tops[1][0]
            z = jnp.sum(jnp.where(cand >= tau, jnp.exp(cand - cmax), 0.0), axis=0, keepdims=True)
            tau_s[h:h + 1, :] = tau
            c0_s[h] = jnp.exp(s0_s[h] - tops[0][0]) / z
            p1_s[h] = jnp.exp(s1_s[h] - tops[1][0])
        acc_s[...] = jnp.zeros_like(acc_s)

    a_t = _nt_dot(u_ref[...], xn_ref[...])
    for j in range(PEER_TE // PEER_NKEYS):
        i1 = e * (PEER_TE // PEER_NKEYS) + j
        g = None
        for h in range(PEER_HEADS):
            tsum = s1_s[h] + s0_s[h, pl.ds(i1, 1), :]
            gh = jnp.where(tsum >= tau_s[h:h + 1, :], p1_s[h] * c0_s[h, pl.ds(i1, 1), :], 0.0)
            g = gh if g is None else g + gh
        a = a_t[j * PEER_NKEYS:(j + 1) * PEER_NKEYS]
        gelu = 0.5 * a * (1.0 + lax.erf(a * (0.5 ** 0.5)))
        w_s[j * PEER_NKEYS:(j + 1) * PEER_NKEYS, :] = (g * gelu).astype(BF)
    acc_s[...] += _dot(vt_ref[...], w_s[...])

    @pl.when(e == pl.num_programs(1) - 1)
    def _fin():
        gate = mod_ref[0, 0][gate_idx:gate_idx + 1]
        out_ref[...] = h_ref[...] + gate * acc_s[...].T


def _peer_residual(h2, q2, xn2, keys, u, vt, modtab, *, gate_idx, tok_per_batch):
    ntok = h2.shape[0]
    nexp = u.shape[0]
    tn, te = PEER_TN, PEER_TE
    tiles_per_batch = tok_per_batch // tn
    ctx_tiles = CTX // tn
    nq = q2.shape[1]

    def mod_map(t, e):
        return (t // tiles_per_batch, jnp.minimum((t % tiles_per_batch) // ctx_tiles, 1), 0, 0)

    return pl.pallas_call(
        functools.partial(_peer_kernel, gate_idx=gate_idx),
        grid=(ntok // tn, nexp // te),
        in_specs=[pl.BlockSpec((tn, nq), lambda t, e: (t, 0)),
                  pl.BlockSpec((tn, D), lambda t, e: (t, 0)),
                  pl.BlockSpec((PEER_HEADS, 2, PEER_NKEYS, PEER_DHALF), lambda t, e: (0, 0, 0, 0)),
                  pl.BlockSpec((te, D), lambda t, e: (e, 0)),
                  pl.BlockSpec((D, te), lambda t, e: (0, e)),
                  pl.BlockSpec((tn, D), lambda t, e: (t, 0)),
                  pl.BlockSpec((1, 1, N_MOD, D), mod_map)],
        out_specs=pl.BlockSpec((tn, D), lambda t, e: (t, 0)),
        out_shape=jax.ShapeDtypeStruct((ntok, D), F32),
        scratch_shapes=[pltpu.VMEM((PEER_HEADS, PEER_NKEYS, tn), F32),
                        pltpu.VMEM((PEER_HEADS, PEER_NKEYS, tn), F32),
                        pltpu.VMEM((PEER_HEADS, PEER_NKEYS, tn), F32),
                        pltpu.VMEM((PEER_HEADS, PEER_NKEYS, tn), F32),
                        pltpu.VMEM((PEER_HEADS, tn), F32),
                        pltpu.VMEM((te, tn), BF),
                        pltpu.VMEM((D, tn), F32)],
        compiler_params=_cparams(("parallel", "arbitrary")),
    )(q2, xn2, keys, u, vt, h2, modtab)


def _final_kernel(h_ref, g_ref, o_ref):
    x = h_ref[0]
    o_ref[0] = x * lax.rsqrt(jnp.mean(x * x, axis=-1, keepdims=True) + EPS) * g_ref[...]


def _final_norm(h, g):
    bsz, t, _ = h.shape
    s = t - CTX
    off = CTX // TM
    return pl.pallas_call(
        _final_kernel,
        grid=(bsz, s // TM),
        in_specs=[pl.BlockSpec((1, TM, D), lambda b, i: (b, i + off, 0)),
                  pl.BlockSpec((1, D), lambda b, i: (0, 0))],
        out_specs=pl.BlockSpec((1, TM, D), lambda b, i: (b, i, 0)),
        out_shape=jax.ShapeDtypeStruct((bsz, s, D), F32),
        compiler_params=_cparams(("parallel", "parallel")),
    )(h, g.reshape(1, D))


def _rot_half_cols(w, dh):
    d, n = w.shape
    wh = w.reshape(d, n // dh, 2, dh // 2)
    return jnp.concatenate([-wh[:, :, 1], wh[:, :, 0]], axis=2).reshape(d, n)


def _rope_tables(s, dh, width):
    rows = s // GRID_W
    r = jnp.repeat(jnp.arange(rows, dtype=F32), GRID_W)
    col = jnp.broadcast_to(jnp.arange(GRID_W, dtype=F32), (rows, GRID_W)).reshape(-1)
    nf = dh // 4
    inv = ROPE_THETA ** (-jnp.arange(nf, dtype=F32) / nf)
    ang = jnp.concatenate([r[:, None] * inv, col[:, None] * inv], axis=-1)
    cos = jnp.concatenate([jnp.ones((CTX, dh // 2), F32), jnp.cos(ang)], axis=0)
    sin = jnp.concatenate([jnp.zeros((CTX, dh // 2), F32), jnp.sin(ang)], axis=0)
    reps = width // (dh // 2)
    return jnp.tile(cos, (1, reps)), jnp.tile(sin, (1, reps))


def _interleave_heads(a, nh, dh):
    lead = a.shape[:-1]
    return a.reshape(lead + (2, nh, dh)).swapaxes(-3, -2).reshape(lead + (2 * nh * dh,))


def kernel(x, c, ctx, c_ctx, mod_w, mod_b, norm1_g, norm2_g, ev_w_in, ev_ml_conv_w, ev_ml_conv_b,
           ev_ml_gate_b, ev_ml_norm_g, ev_da_lam, ev_da_norm_g, ev_w_out, od_w_in, od_sink, od_w_out,
           pr_w_q, pr_keys, pr_u, pr_v, final_g):
    bsz, s, _ = x.shape
    depth = mod_w.shape[0]
    t = CTX + s
    ml_w = ML_HEADS * ML_DV
    da_w = DA_HEADS * DA_DV
    qk_w = 2 * ML_HEADS * ML_DQK
    da_qw = DA_HEADS * 2 * DA_DQK

    h = jnp.concatenate([ctx, x], axis=1)

    rows = -(-(bsz + 1) // 8) * 8
    cc = jnp.zeros((rows, D), F32).at[:bsz].set(c).at[bsz].set(c_ctx)
    mods = _modulation(cc, mod_w, mod_b)
    mod_lat = mods[:, :bsz].reshape(depth, bsz, N_MOD, D)
    mod_ctx = jnp.broadcast_to(mods[:, bsz].reshape(depth, 1, N_MOD, D), (depth, bsz, N_MOD, D))
    modtabs = jnp.stack([mod_ctx, mod_lat], axis=2)

    wa_rope_w = (WA_HEADS + WA_KV_HEADS) * WA_DH
    cos_t, sin_t = _rope_tables(s, WA_DH, wa_rope_w)

    for i in range(depth):
        j = i // 2
        modtab = modtabs[i]
        if i % 2 == 0:
            lam_init = 0.8 - 0.6 * math.exp(-0.3 * i)
            w = ev_w_in[j]
            o0 = 0
            w_mqk = w[:, o0:o0 + qk_w]; o0 += qk_w
            w_mv = w[:, o0:o0 + ml_w]; o0 += ml_w
            w_mo = w[:, o0:o0 + ml_w]; o0 += ml_w
            w_g = w[:, o0:o0 + 4 * ML_HEADS]; o0 += 4 * ML_HEADS
            w_dq = w[:, o0:o0 + da_qw]; o0 += da_qw
            w_dk = w[:, o0:o0 + da_qw]; o0 += da_qw
            w_dv = w[:, o0:o0 + da_w]
            w_rope = jnp.concatenate([w_dq, w_dk], axis=1)
            gpad = jnp.zeros((D, 128 - 4 * ML_HEADS), F32)
            wm = jnp.concatenate([w_rope, _interleave_heads(w_mqk, ML_HEADS, ML_DQK), w_mv, w_mo, w_dv, w_g, gpad],
                                 axis=1).astype(BF)
            wr = _rot_half_cols(w_rope, DA_DQK).astype(BF)
            nr = 2 * da_qw
            col_mqk = nr
            col_mv = col_mqk + qk_w
            col_mo = col_mv + ml_w
            col_dv = col_mo + ml_w
            col_g = col_dv + da_w
            (y,) = _norm_mod_proj(h, modtab, norm1_g[i], wm, wr, cos_t[:, :nr], sin_t[:, :nr],
                                  shift_idx=0, scale_idx=1)
            gates_t = jnp.swapaxes(y[:, :, col_g:col_g + 4 * ML_HEADS], 1, 2)
            gk, qs = _gate_scan(gates_t, ev_ml_gate_b[j])
            qs = qs.reshape(bsz, 4, ML_HEADS, t).transpose(0, 2, 3, 1)
            ml = _mlstm(y, gk, qs, _interleave_heads(ev_ml_conv_w[j], ML_HEADS, ML_DQK),
                        _interleave_heads(ev_ml_conv_b[j], ML_HEADS, ML_DQK), ev_ml_norm_g[j],
                        col_qk=col_mqk, col_v=col_mv, col_o=col_mo)
            da = _diff_attention(y, ev_da_lam[j], ev_da_norm_g[j], col_q=0, col_k=da_qw, col_v=col_dv,
                                 lam_init=lam_init)
            w_out = ev_w_out[j].astype(BF)
            h = _outproj_residual(h, modtab, [(ml, w_out[:ml_w]), (da, w_out[ml_w:])], gate_idx=2)
        else:
            w = od_w_in[j]
            wm = w.astype(BF)
            wr = _rot_half_cols(w[:, :wa_rope_w], WA_DH).astype(BF)
            (y,) = _norm_mod_proj(h, modtab, norm1_g[i], wm, wr, cos_t, sin_t, shift_idx=0, scale_idx=1)
            att = _window_attention(y, od_sink[j], col_q=0, col_k=WA_HEADS * WA_DH,
                                    col_v=WA_HEADS * WA_DH + WA_KV_HEADS * WA_DH)
            h = _outproj_residual(h, modtab, [(att, od_w_out[j].astype(BF))], gate_idx=2)
        q, xn = _norm_mod_proj(h, modtab, norm2_g[i], pr_w_q[i].astype(BF), shift_idx=3, scale_idx=4,
                               want_xn=True)
        h = _peer_residual(h.reshape(bsz * t, D), q.reshape(bsz * t, -1), xn.reshape(bsz * t, D),
                           pr_keys[i].astype(BF), pr_u[i].astype(BF), pr_v[i].T.astype(BF), modtab,
                           gate_idx=5, tok_per_batch=t).reshape(bsz, t, D)
    return _final_norm(h, final_g)
```

```python
import functools
import math

import jax
import jax.numpy as jnp
from jax import lax
from jax.experimental import pallas as pl
from jax.experimental.pallas import tpu as pltpu

D = 1024
CTX = 256
GRID_W = 64
EPS = 1e-6
ROPE_THETA = 10000.0
N_MOD = 6

ML_HEADS = 4
ML_DQK = 64
ML_DV = 128
DA_HEADS = 4
DA_DQK = 64
DA_DV = 128
WA_HEADS = 16
WA_KV_HEADS = 4
WA_GROUP = 4
WA_DH = 64
WINDOW = 128

PEER_HEADS = 8
PEER_NKEYS = 128
PEER_DHALF = 128
PEER_TOPK = 16

TM = 256
TQ_ML = 256
TQ_DA = 256
TQ_WA = 128
PEER_TN = 256
PEER_TE = 1024
GS_PITCH = 136
VMEM_LIMIT = 56 * 1024 * 1024

BF = jnp.bfloat16
F32 = jnp.float32
NEG_INF = float("-inf")


def _cparams(sem):
    return pltpu.CompilerParams(dimension_semantics=sem, vmem_limit_bytes=VMEM_LIMIT)


def _nt_dot(a, b):
    return lax.dot_general(a, b, (((1,), (1,)), ((), ())), preferred_element_type=F32)


def _dot(a, b):
    return jnp.dot(a, b, preferred_element_type=F32)


def _mod_kernel(cc_ref, w_ref, b_ref, o_ref):
    a = jax.nn.silu(cc_ref[...]).astype(BF)
    o_ref[0] = _dot(a, w_ref[0].astype(BF)) + b_ref[0]


def _modulation(cc, mod_w, mod_b):
    depth = mod_w.shape[0]
    rows = cc.shape[0]
    tn = 1536
    return pl.pallas_call(
        _mod_kernel,
        grid=(depth, N_MOD * D // tn),
        in_specs=[pl.BlockSpec((rows, D), lambda l, n: (0, 0)),
                  pl.BlockSpec((1, D, tn), lambda l, n: (l, 0, n)),
                  pl.BlockSpec((1, 1, tn), lambda l, n: (l, 0, n))],
        out_specs=pl.BlockSpec((1, rows, tn), lambda l, n: (l, 0, n)),
        out_shape=jax.ShapeDtypeStruct((depth, rows, N_MOD * D), F32),
        compiler_params=_cparams(("parallel", "parallel")),
    )(cc, mod_w, mod_b.reshape(depth, 1, N_MOD * D))


def _mod_spec():
    return pl.BlockSpec((1, 1, N_MOD, D), lambda b, t: (b, jnp.minimum(t // (CTX // TM), 1), 0, 0))


def _proj_kernel(*refs, shift_idx, scale_idx, nrope, want_xn):
    it = iter(refs)
    h_ref, mod_ref, g_ref, wm_ref = next(it), next(it), next(it), next(it)
    if nrope:
        wr_ref, cos_ref, sin_ref = next(it), next(it), next(it)
    y_ref = next(it)
    x = h_ref[0]
    xn = x * lax.rsqrt(jnp.mean(x * x, axis=-1, keepdims=True) + EPS) * g_ref[...]
    mod = mod_ref[0, 0]
    xm = xn * (1.0 + mod[scale_idx:scale_idx + 1]) + mod[shift_idx:shift_idx + 1]
    xb = xm.astype(BF)
    acc = _dot(xb, wm_ref[...])
    if nrope:
        rot = _dot(xb, wr_ref[...])
        y_ref[0, :, :nrope] = acc[:, :nrope] * cos_ref[...] + rot * sin_ref[...]
        y_ref[0, :, nrope:] = acc[:, nrope:]
    else:
        y_ref[0] = acc
    if want_xn:
        next(it)[0] = xb


def _norm_mod_proj(h, modtab, g, wm, wr=None, cos=None, sin=None, *, shift_idx, scale_idx, want_xn=False):
    bsz, t, _ = h.shape
    nm = wm.shape[1]
    nrope = 0 if wr is None else wr.shape[1]
    in_specs = [pl.BlockSpec((1, TM, D), lambda b, i: (b, i, 0)),
                _mod_spec(),
                pl.BlockSpec((1, D), lambda b, i: (0, 0)),
                pl.BlockSpec((D, nm), lambda b, i: (0, 0))]
    args = [h, modtab, g.reshape(1, D), wm]
    if nrope:
        in_specs += [pl.BlockSpec((D, nrope), lambda b, i: (0, 0)),
                     pl.BlockSpec((TM, nrope), lambda b, i: (i, 0)),
                     pl.BlockSpec((TM, nrope), lambda b, i: (i, 0))]
        args += [wr, cos, sin]
    out_specs = [pl.BlockSpec((1, TM, nm), lambda b, i: (b, i, 0))]
    out_shape = [jax.ShapeDtypeStruct((bsz, t, nm), F32)]
    if want_xn:
        out_specs.append(pl.BlockSpec((1, TM, D), lambda b, i: (b, i, 0)))
        out_shape.append(jax.ShapeDtypeStruct((bsz, t, D), BF))
    return pl.pallas_call(
        functools.partial(_proj_kernel, shift_idx=shift_idx, scale_idx=scale_idx, nrope=nrope, want_xn=want_xn),
        grid=(bsz, t // TM), in_specs=in_specs, out_specs=out_specs, out_shape=out_shape,
        compiler_params=_cparams(("parallel", "parallel")),
    )(*args)


def _outproj_kernel(*refs, gate_idx, n_src):
    h_ref, mod_ref = refs[0], refs[1]
    o_ref = refs[2 + 2 * n_src]
    acc = None
    for s in range(n_src):
        part = _dot(refs[2 + 2 * s][0].astype(BF), refs[3 + 2 * s][...])
        acc = part if acc is None else acc + part
    gate = mod_ref[0, 0][gate_idx:gate_idx + 1]
    o_ref[0] = h_ref[0] + gate * acc


def _outproj_residual(h, modtab, srcs, *, gate_idx):
    bsz, t, _ = h.shape
    in_specs = [pl.BlockSpec((1, TM, D), lambda b, i: (b, i, 0)), _mod_spec()]
    args = [h, modtab]
    for x, w in srcs:
        k = x.shape[-1]
        in_specs += [pl.BlockSpec((1, TM, k), lambda b, i: (b, i, 0)),
                     pl.BlockSpec((k, D), lambda b, i: (0, 0))]
        args += [x, w]
    return pl.pallas_call(
        functools.partial(_outproj_kernel, gate_idx=gate_idx, n_src=len(srcs)),
        grid=(bsz, t // TM), in_specs=in_specs,
        out_specs=pl.BlockSpec((1, TM, D), lambda b, i: (b, i, 0)),
        out_shape=jax.ShapeDtypeStruct((bsz, t, D), F32),
        compiler_params=_cparams(("parallel", "parallel")),
    )(*args)


def _lane_scan(x, op, ident, lane, reverse):
    for k in range(7):
        s = 1 << k
        if reverse:
            sh = pltpu.roll(x, 128 - s, 1)
            x = op(x, jnp.where(lane < 128 - s, sh, ident))
        else:
            sh = pltpu.roll(x, s, 1)
            x = op(x, jnp.where(lane >= s, sh, ident))
    return x


def _gate_scan_kernel(g_ref, b_ref, gk_ref, qs_ref, *, t):
    nb = t // 128
    nh = ML_HEADS
    lane = lax.broadcasted_iota(jnp.int32, (nh, 128), 1)
    pre = g_ref[0] + b_ref[...]
    for d in range(2):
        ig = pre[(2 * d) * nh:(2 * d + 1) * nh]
        lf = jax.nn.log_sigmoid(pre[(2 * d + 1) * nh:(2 * d + 2) * nh])
        if d == 0:
            order = list(range(nb))
        else:
            order = list(range(CTX // 128 - 1, -1, -1)) + list(range(nb - 1, CTX // 128 - 1, -1))
        carry_b = jnp.zeros((nh, 1), F32)
        carry_m = jnp.full((nh, 1), NEG_INF, F32)
        edge = 0 if d else 127
        for blk in order:
            sl = slice(blk * 128, (blk + 1) * 128)
            bb = _lane_scan(lf[:, sl], jnp.add, 0.0, lane, bool(d)) + carry_b
            gg = ig[:, sl] - bb
            mm = jnp.maximum(_lane_scan(gg, jnp.maximum, NEG_INF, lane, bool(d)), carry_m)
            carry_b = bb[:, edge:edge + 1]
            carry_m = mm[:, edge:edge + 1]
            gk_ref[0, d * nh:(d + 1) * nh, sl] = gg
            qs_ref[0, (2 * d) * nh:(2 * d + 1) * nh, sl] = mm
            qs_ref[0, (2 * d + 1) * nh:(2 * d + 2) * nh, sl] = bb + mm


def _gate_scan(gates_t, gate_b):
    bsz, ng, t = gates_t.shape
    return pl.pallas_call(
        functools.partial(_gate_scan_kernel, t=t),
        grid=(bsz,),
        in_specs=[pl.BlockSpec((1, ng, t), lambda b: (b, 0, 0)),
                  pl.BlockSpec((ng, 1), lambda b: (0, 0))],
        out_specs=[pl.BlockSpec((1, 2 * ML_HEADS, t), lambda b: (b, 0, 0)),
                   pl.BlockSpec((1, 4 * ML_HEADS, t), lambda b: (b, 0, 0))],
        out_shape=[jax.ShapeDtypeStruct((bsz, 2 * ML_HEADS, t), F32),
                   jax.ShapeDtypeStruct((bsz, 4 * ML_HEADS, t), F32)],
        compiler_params=_cparams(("parallel",)),
    )(gates_t, gate_b.reshape(ng, 1))


def _mlstm_kernel(qk_ref, v_ref, o_ref, gk_ref, qs_ref, cw_ref, cb_ref, ng_ref, out_ref,
                  q_s, k_s, v_s, *, t):
    hd = pl.program_id(1)
    x = qk_ref[0]
    row = lax.broadcasted_iota(jnp.int32, (t, 1), 0)
    xp = jnp.where((row == 0) | (row == CTX), 0.0, pltpu.roll(x, 1, 0))
    xn = jnp.where((row == CTX - 1) | (row == t - 1), 0.0, pltpu.roll(x, t - 1, 0))
    cw = cw_ref[...]
    y = cb_ref[...] + xp * cw[0:1] + x * cw[1:2] + xn * cw[2:3]
    y = jax.nn.silu(y)
    q_s[...] = y[:, :ML_DQK].astype(BF)
    k_s[...] = (y[:, ML_DQK:] * (ML_DQK ** -0.5)).astype(BF)
    v_s[...] = v_ref[0].astype(BF)
    col = lax.broadcasted_iota(jnp.int32, (1, t), 1)
    col_b = jnp.where(col < CTX, CTX - 1 - col, t - 1 + CTX - col)

    def q_block(qi, carry):
        r0 = pl.multiple_of(qi * TQ_ML, TQ_ML)
        qb = q_s[pl.ds(r0, TQ_ML), :]
        s = _nt_dot(qb, k_s[...])
        rowa = r0 + lax.broadcasted_iota(jnp.int32, (TQ_ML, 1), 0)
        qs = qs_ref[0, 0, pl.ds(r0, TQ_ML), :]
        hsum = None
        for d in range(2):
            if d == 0:
                allowed = col <= rowa
            else:
                allowed = col_b <= jnp.where(rowa < CTX, CTX - 1 - rowa, t - 1 + CTX - rowa)
            gk = gk_ref[0, pl.ds(d * ML_HEADS + hd, 1), :]
            p = s * jnp.exp(jnp.where(allowed, gk - qs[:, 2 * d:2 * d + 1], NEG_INF))
            den = jnp.sum(p, axis=-1, keepdims=True)
            num = _dot(p.astype(BF), v_s[...])
            hdir = num / jnp.maximum(jnp.abs(den), jnp.exp(-qs[:, 2 * d + 1:2 * d + 2]))
            hsum = hdir if hsum is None else hsum + hdir
        yn = hsum * lax.rsqrt(jnp.mean(hsum * hsum, axis=-1, keepdims=True) + EPS) * ng_ref[...]
        out_ref[0, pl.ds(r0, TQ_ML), :] = yn * jax.nn.sigmoid(o_ref[0, pl.ds(r0, TQ_ML), :])
        return carry

    lax.fori_loop(0, t // TQ_ML, q_block, 0)


def _mlstm(y, gk, qs, conv_w, conv_b, norm_g, *, col_qk, col_v, col_o):
    bsz, t, _ = y.shape
    w = ML_HEADS * ML_DV
    return pl.pallas_call(
        functools.partial(_mlstm_kernel, t=t),
        grid=(bsz, ML_HEADS),
        in_specs=[pl.BlockSpec((1, t, 128), lambda b, h: (b, 0, col_qk // 128 + h)),
                  pl.BlockSpec((1, t, 128), lambda b, h: (b, 0, col_v // 128 + h)),
                  pl.BlockSpec((1, t, 128), lambda b, h: (b, 0, col_o // 128 + h)),
                  pl.BlockSpec((1, 2 * ML_HEADS, t), lambda b, h: (b, 0, 0)),
                  pl.BlockSpec((1, 1, t, 4), lambda b, h: (b, h, 0, 0)),
                  pl.BlockSpec((3, 128), lambda b, h: (0, h)),
                  pl.BlockSpec((1, 128), lambda b, h: (0, h)),
                  pl.BlockSpec((1, 128), lambda b, h: (0, h))],
        out_specs=pl.BlockSpec((1, t, 128), lambda b, h: (b, 0, h)),
        out_shape=jax.ShapeDtypeStruct((bsz, t, w), F32),
        scratch_shapes=[pltpu.VMEM((t, ML_DQK), BF), pltpu.VMEM((t, ML_DQK), BF), pltpu.VMEM((t, ML_DV), BF)],
        compiler_params=_cparams(("parallel", "parallel")),
    )(y, y, y, gk, qs, conv_w, conv_b.reshape(1, w), norm_g.reshape(1, w))


def _da_kernel(q_ref, k_ref, v_ref, lam_ref, ng_ref, out_ref, *, t, lam_init):
    qt = pl.program_id(2)
    lp = lam_ref[...]
    lam = (jnp.exp(jnp.sum(lp[0:1] * lp[1:2], axis=-1, keepdims=True))
           - jnp.exp(jnp.sum(lp[2:3] * lp[3:4], axis=-1, keepdims=True)) + lam_init)
    q = q_ref[0] * (DA_DQK ** -0.5)

    def attend(nk):
        v = v_ref[0, :nk, :].astype(BF)
        outs = []
        for m in range(2):
            qm = q[:, m * DA_DQK:(m + 1) * DA_DQK].astype(BF)
            km = k_ref[0, :nk, m * DA_DQK:(m + 1) * DA_DQK].astype(BF)
            s = _nt_dot(qm, km)
            e = jnp.exp(s - jnp.max(s, axis=-1, keepdims=True))
            outs.append(_dot(e.astype(BF), v) / jnp.sum(e, axis=-1, keepdims=True))
        o = outs[0] - lam * outs[1]
        yn = o * lax.rsqrt(jnp.mean(o * o, axis=-1, keepdims=True) + EPS) * ng_ref[...]
        out_ref[0] = yn * (1.0 - lam_init)

    @pl.when(qt < CTX // TQ_DA)
    def _():
        attend(CTX)

    @pl.when(qt >= CTX // TQ_DA)
    def _():
        attend(t)


def _diff_attention(y, da_lam, norm_g, *, col_q, col_k, col_v, lam_init):
    bsz, t, _ = y.shape
    w = DA_HEADS * DA_DV
    return pl.pallas_call(
        functools.partial(_da_kernel, t=t, lam_init=lam_init),
        grid=(bsz, DA_HEADS, t // TQ_DA),
        in_specs=[pl.BlockSpec((1, TQ_DA, 128), lambda b, h, i: (b, i, col_q // 128 + h)),
                  pl.BlockSpec((1, t, 128), lambda b, h, i: (b, 0, col_k // 128 + h)),
                  pl.BlockSpec((1, t, 128), lambda b, h, i: (b, 0, col_v // 128 + h)),
                  pl.BlockSpec((4, DA_DQK), lambda b, h, i: (0, 0)),
                  pl.BlockSpec((1, 128), lambda b, h, i: (0, h))],
        out_specs=pl.BlockSpec((1, TQ_DA, 128), lambda b, h, i: (b, i, h)),
        out_shape=jax.ShapeDtypeStruct((bsz, t, w), F32),
        compiler_params=_cparams(("parallel", "parallel", "parallel")),
    )(y, y, y, da_lam, norm_g.reshape(1, w))


def _wa_kernel(q_ref, k_ref, v_ref, sink_ref, out_ref, *, t):
    i = pl.program_id(1)
    n_ctx_blk = CTX // TQ_WA
    n_lat = t - CTX
    span = 3 * WINDOW

    def attend(local):
        if local:
            lat0 = jnp.clip((i - n_ctx_blk - 1) * TQ_WA, 0, n_lat - span)
            start = pl.multiple_of(CTX + lat0, TQ_WA)
            qpos = (i - n_ctx_blk) * TQ_WA + lax.broadcasted_iota(jnp.int32, (TQ_WA, 1), 0)
            kpos = lat0 + lax.broadcasted_iota(jnp.int32, (1, span), 1)
            mask = jnp.abs(qpos - kpos) <= WINDOW
        for g in range(WA_KV_HEADS):
            cs = slice(g * WA_DH, (g + 1) * WA_DH)
            kc = k_ref[0, :CTX, cs].astype(BF)
            vc = v_ref[0, :CTX, cs].astype(BF)
            if local:
                kl = k_ref[0, pl.ds(start, span), cs].astype(BF)
                vl = v_ref[0, pl.ds(start, span), cs].astype(BF)
            outs = []
            for hh in range(WA_GROUP):
                hd = g * WA_GROUP + hh
                q = (q_ref[0, :, hd * WA_DH:(hd + 1) * WA_DH] * (WA_DH ** -0.5)).astype(BF)
                sink = sink_ref[:, hd:hd + 1]
                sc = _nt_dot(q, kc)
                mx = jnp.maximum(jnp.max(sc, axis=-1, keepdims=True), sink)
                if local:
                    sl = jnp.where(mask, _nt_dot(q, kl), NEG_INF)
                    mx = jnp.maximum(mx, jnp.max(sl, axis=-1, keepdims=True))
                ec = jnp.exp(sc - mx)
                den = jnp.sum(ec, axis=-1, keepdims=True) + jnp.exp(sink - mx)
                num = _dot(ec.astype(BF), vc)
                if local:
                    el = jnp.exp(sl - mx)
                    den = den + jnp.sum(el, axis=-1, keepdims=True)
                    num = num + _dot(el.astype(BF), vl)
                outs.append(num / den)
            out_ref[0, :, g * WA_GROUP * WA_DH:(g + 1) * WA_GROUP * WA_DH] = jnp.concatenate(outs, axis=1)

    @pl.when(i < n_ctx_blk)
    def _():
        attend(False)

    @pl.when(i >= n_ctx_blk)
    def _():
        attend(True)


def _window_attention(y, sink, *, col_q, col_k, col_v):
    bsz, t, _ = y.shape
    qw = WA_HEADS * WA_DH
    kw = WA_KV_HEADS * WA_DH
    return pl.pallas_call(
        functools.partial(_wa_kernel, t=t),
        grid=(bsz, t // TQ_WA),
        in_specs=[pl.BlockSpec((1, TQ_WA, qw), lambda b, i: (b, i, col_q // qw)),
                  pl.BlockSpec((1, t, kw), lambda b, i: (b, 0, col_k // kw)),
                  pl.BlockSpec((1, t, kw), lambda b, i: (b, 0, col_v // kw)),
                  pl.BlockSpec((1, WA_HEADS), lambda b, i: (0, 0))],
        out_specs=pl.BlockSpec((1, TQ_WA, qw), lambda b, i: (b, i, 0)),
        out_shape=jax.ShapeDtypeStruct((bsz, t, qw), F32),
        compiler_params=_cparams(("parallel", "parallel")),
    )(y, y, y, sink.reshape(1, WA_HEADS))


def _pop_max(x, rowid):
    m = jnp.max(x, axis=0, keepdims=True)
    first = jnp.min(jnp.where(x == m, rowid, float(x.shape[0])), axis=0, keepdims=True)
    hit = rowid == first
    return m, first, hit, jnp.where(hit, NEG_INF, x)


_CAND = [(a, b) for a in range(PEER_TOPK) for b in range(PEER_TOPK) if (a + 1) * (b + 1) <= PEER_TOPK]


def _peer_kernel(q_ref, xn_ref, keys_ref, ut_ref, v_ref, h_ref, mod_ref, out_ref,
                 i1_s, i2_s, gt_s, i1t_s, i2t_s, gtt_s, gs_s, w_s, acc_s, *, gate_idx):
    e = pl.program_id(1)
    tn = q_ref.shape[0]
    nslot = PEER_HEADS * PEER_TOPK

    @pl.when(e == 0)
    def _route():
        rowid = lax.broadcasted_iota(jnp.int32, (PEER_NKEYS, tn), 0).astype(F32)
        crow = lax.broadcasted_iota(jnp.int32, (len(_CAND), tn), 0).astype(F32)
        for h in range(PEER_HEADS):
            tops, idxs = [], []
            for p in range(2):
                c0 = (2 * h + p) * PEER_DHALF
                qhp = q_ref[:, c0:c0 + PEER_DHALF].astype(BF)
                s = _nt_dot(keys_ref[h, p], qhp)
                vals, ids = [], []
                for _ in range(PEER_TOPK):
                    m, first, _, s = _pop_max(s, rowid)
                    vals.append(m)
                    ids.append(first)
                tops.append(vals)
                idxs.append(ids)
            work = jnp.concatenate([tops[0][a] + tops[1][b] for a, b in _CAND], axis=0)
            ci1 = jnp.concatenate([idxs[0][a] for a, _ in _CAND], axis=0)
            ci2 = jnp.concatenate([idxs[1][b] for _, b in _CAND], axis=0)
            sv, s1, s2 = [], [], []
            for _ in range(PEER_TOPK):
                m, _, hit, work = _pop_max(work, crow)
                sv.append(m)
                s1.append(jnp.max(jnp.where(hit, ci1, -1.0), axis=0, keepdims=True))
                s2.append(jnp.max(jnp.where(hit, ci2, -1.0), axis=0, keepdims=True))
            ev = jnp.exp(jnp.concatenate(sv, axis=0) - sv[0])
            rows = slice(h * PEER_TOPK, (h + 1) * PEER_TOPK)
            i1_s[rows, :] = jnp.concatenate(s1, axis=0)
            i2_s[rows, :] = jnp.concatenate(s2, axis=0)
            gt_s[rows, :] = ev / jnp.sum(ev, axis=0, keepdims=True)
        i1t_s[...] = i1_s[...].T
        i2t_s[...] = i2_s[...].T
        gtt_s[...] = gt_s[...].T
        sub = lax.broadcasted_iota(jnp.int32, (PEER_NKEYS, nslot), 0).astype(F32)

        def token(n, carry):
            at = jnp.where(sub == i1t_s[pl.ds(n, 1), :], gtt_s[pl.ds(n, 1), :], 0.0).astype(BF)
            bt = jnp.where(sub == i2t_s[pl.ds(n, 1), :], 1.0, 0.0).astype(BF)
            gs_s[pl.ds(pl.multiple_of(n * GS_PITCH, 8), PEER_NKEYS), :] = _nt_dot(at, bt)
            return carry

        lax.fori_loop(0, tn, token, 0, unroll=16)

    a = _dot(xn_ref[...], ut_ref[...])
    for j in range(PEER_TE // PEER_NKEYS):
        i1 = e * (PEER_TE // PEER_NKEYS) + j
        g = gs_s[pl.ds(i1, tn, stride=GS_PITCH), :]
        aj = a[:, j * PEER_NKEYS:(j + 1) * PEER_NKEYS]
        gelu = 0.5 * aj * (1.0 + lax.erf(aj * (0.5 ** 0.5)))
        w_s[:, j * PEER_NKEYS:(j + 1) * PEER_NKEYS] = (g * gelu).astype(BF)
    part = _dot(w_s[...], v_ref[...])

    @pl.when(e == 0)
    def _first():
        acc_s[...] = part

    @pl.when(e > 0)
    def _rest():
        acc_s[...] += part

    @pl.when(e == pl.num_programs(1) - 1)
    def _fin():
        gate = mod_ref[0, 0][gate_idx:gate_idx + 1]
        out_ref[...] = h_ref[...] + gate * acc_s[...]


def _peer_residual(h2, q2, xn2, keys, ut, v, modtab, *, gate_idx, tok_per_batch):
    ntok = h2.shape[0]
    nexp = v.shape[0]
    nslot = PEER_HEADS * PEER_TOPK
    tn, te = PEER_TN, PEER_TE
    tiles_per_batch = tok_per_batch // tn
    ctx_tiles = CTX // tn
    nq = q2.shape[1]

    def mod_map(t, e):
        return (t // tiles_per_batch, jnp.minimum((t % tiles_per_batch) // ctx_tiles, 1), 0, 0)

    return pl.pallas_call(
        functools.partial(_peer_kernel, gate_idx=gate_idx),
        grid=(ntok // tn, nexp // te),
        in_specs=[pl.BlockSpec((tn, nq), lambda t, e: (t, 0)),
                  pl.BlockSpec((tn, D), lambda t, e: (t, 0)),
                  pl.BlockSpec((PEER_HEADS, 2, PEER_NKEYS, PEER_DHALF), lambda t, e: (0, 0, 0, 0)),
                  pl.BlockSpec((D, te), lambda t, e: (0, e)),
                  pl.BlockSpec((te, D), lambda t, e: (e, 0)),
                  pl.BlockSpec((tn, D), lambda t, e: (t, 0)),
                  pl.BlockSpec((1, 1, N_MOD, D), mod_map)],
        out_specs=pl.BlockSpec((tn, D), lambda t, e: (t, 0)),
        out_shape=jax.ShapeDtypeStruct((ntok, D), F32),
        scratch_shapes=[pltpu.VMEM((nslot, tn), F32),
                        pltpu.VMEM((nslot, tn), F32),
                        pltpu.VMEM((nslot, tn), F32),
                        pltpu.VMEM((tn, nslot), F32),
                        pltpu.VMEM((tn, nslot), F32),
                        pltpu.VMEM((tn, nslot), F32),
                        pltpu.VMEM((tn * GS_PITCH, PEER_NKEYS), F32),
                        pltpu.VMEM((tn, te), BF),
                        pltpu.VMEM((tn, D), F32)],
        compiler_params=_cparams(("parallel", "arbitrary")),
    )(q2, xn2, keys, ut, v, h2, modtab)


def _final_kernel(h_ref, g_ref, o_ref):
    x = h_ref[0]
    o_ref[0] = x * lax.rsqrt(jnp.mean(x * x, axis=-1, keepdims=True) + EPS) * g_ref[...]


def _final_norm(h, g):
    bsz, t, _ = h.shape
    s = t - CTX
    off = CTX // TM
    return pl.pallas_call(
        _final_kernel,
        grid=(bsz, s // TM),
        in_specs=[pl.BlockSpec((1, TM, D), lambda b, i: (b, i + off, 0)),
                  pl.BlockSpec((1, D), lambda b, i: (0, 0))],
        out_specs=pl.BlockSpec((1, TM, D), lambda b, i: (b, i, 0)),
        out_shape=jax.ShapeDtypeStruct((bsz, s, D), F32),
        compiler_params=_cparams(("parallel", "parallel")),
    )(h, g.reshape(1, D))


def _rot_half_cols(w, dh):
    d, n = w.shape
    wh = w.reshape(d, n // dh, 2, dh // 2)
    return jnp.concatenate([-wh[:, :, 1], wh[:, :, 0]], axis=2).reshape(d, n)


def _rope_tables(s, dh, width):
    rows = s // GRID_W
    r = jnp.repeat(jnp.arange(rows, dtype=F32), GRID_W)
    col = jnp.broadcast_to(jnp.arange(GRID_W, dtype=F32), (rows, GRID_W)).reshape(-1)
    nf = dh // 4
    inv = ROPE_THETA ** (-jnp.arange(nf, dtype=F32) / nf)
    ang = jnp.concatenate([r[:, None] * inv, col[:, None] * inv], axis=-1)
    cos = jnp.concatenate([jnp.ones((CTX, dh // 2), F32), jnp.cos(ang)], axis=0)
    sin = jnp.concatenate([jnp.zeros((CTX, dh // 2), F32), jnp.sin(ang)], axis=0)
    reps = width // (dh // 2)
    return jnp.tile(cos, (1, reps)), jnp.tile(sin, (1, reps))


def _interleave_heads(a, nh, dh):
    lead = a.shape[:-1]
    return a.reshape(lead + (2, nh, dh)).swapaxes(-3, -2).reshape(lead + (2 * nh * dh,))


def kernel(x, c, ctx, c_ctx, mod_w, mod_b, norm1_g, norm2_g, ev_w_in, ev_ml_conv_w, ev_ml_conv_b,
           ev_ml_gate_b, ev_ml_norm_g, ev_da_lam, ev_da_norm_g, ev_w_out, od_w_in, od_sink, od_w_out,
           pr_w_q, pr_keys, pr_u, pr_v, final_g):
    bsz, s, _ = x.shape
    depth = mod_w.shape[0]
    t = CTX + s
    ml_w = ML_HEADS * ML_DV
    da_w = DA_HEADS * DA_DV
    qk_w = 2 * ML_HEADS * ML_DQK
    da_qw = DA_HEADS * 2 * DA_DQK

    h = jnp.concatenate([ctx, x], axis=1)

    rows = -(-(bsz + 1) // 8) * 8
    cc = jnp.zeros((rows, D), F32).at[:bsz].set(c).at[bsz].set(c_ctx)
    mods = _modulation(cc, mod_w, mod_b)
    mod_lat = mods[:, :bsz].reshape(depth, bsz, N_MOD, D)
    mod_ctx = jnp.broadcast_to(mods[:, bsz].reshape(depth, 1, N_MOD, D), (depth, bsz, N_MOD, D))
    modtabs = jnp.stack([mod_ctx, mod_lat], axis=2)

    wa_rope_w = (WA_HEADS + WA_KV_HEADS) * WA_DH
    cos_t, sin_t = _rope_tables(s, WA_DH, wa_rope_w)

    for i in range(depth):
        j = i // 2
        modtab = modtabs[i]
        if i % 2 == 0:
            lam_init = 0.8 - 0.6 * math.exp(-0.3 * i)
            w = ev_w_in[j]
            o0 = 0
            w_mqk = w[:, o0:o0 + qk_w]; o0 += qk_w
            w_mv = w[:, o0:o0 + ml_w]; o0 += ml_w
            w_mo = w[:, o0:o0 + ml_w]; o0 += ml_w
            w_g = w[:, o0:o0 + 4 * ML_HEADS]; o0 += 4 * ML_HEADS
            w_dq = w[:, o0:o0 + da_qw]; o0 += da_qw
            w_dk = w[:, o0:o0 + da_qw]; o0 += da_qw
            w_dv = w[:, o0:o0 + da_w]
            w_rope = jnp.concatenate([w_dq, w_dk], axis=1)
            gpad = jnp.zeros((D, 128 - 4 * ML_HEADS), F32)
            wm = jnp.concatenate([w_rope, _interleave_heads(w_mqk, ML_HEADS, ML_DQK), w_mv, w_mo, w_dv, w_g, gpad],
                                 axis=1).astype(BF)
            wr = _rot_half_cols(w_rope, DA_DQK).astype(BF)
            nr = 2 * da_qw
            col_mqk = nr
            col_mv = col_mqk + qk_w
            col_mo = col_mv + ml_w
            col_dv = col_mo + ml_w
            col_g = col_dv + da_w
            (y,) = _norm_mod_proj(h, modtab, norm1_g[i], wm, wr, cos_t[:, :nr], sin_t[:, :nr],
                                  shift_idx=0, scale_idx=1)
            gates_t = jnp.swapaxes(y[:, :, col_g:col_g + 4 * ML_HEADS], 1, 2)
            gk, qs = _gate_scan(gates_t, ev_ml_gate_b[j])
            qs = qs.reshape(bsz, 4, ML_HEADS, t).transpose(0, 2, 3, 1)
            ml = _mlstm(y, gk, qs, _interleave_heads(ev_ml_conv_w[j], ML_HEADS, ML_DQK),
                        _interleave_heads(ev_ml_conv_b[j], ML_HEADS, ML_DQK), ev_ml_norm_g[j],
                        col_qk=col_mqk, col_v=col_mv, col_o=col_mo)
            da = _diff_attention(y, ev_da_lam[j], ev_da_norm_g[j], col_q=0, col_k=da_qw, col_v=col_dv,
                                 lam_init=lam_init)
            w_out = ev_w_out[j].astype(BF)
            h = _outproj_residual(h, modtab, [(ml, w_out[:ml_w]), (da, w_out[ml_w:])], gate_idx=2)
        else:
            w = od_w_in[j]
            wm = w.astype(BF)
            wr = _rot_half_cols(w[:, :wa_rope_w], WA_DH).astype(BF)
            (y,) = _norm_mod_proj(h, modtab, norm1_g[i], wm, wr, cos_t, sin_t, shift_idx=0, scale_idx=1)
            att = _window_attention(y, od_sink[j], col_q=0, col_k=WA_HEADS * WA_DH,
                                    col_v=WA_HEADS * WA_DH + WA_KV_HEADS * WA_DH)
            h = _outproj_residual(h, modtab, [(att, od_w_out[j].astype(BF))], gate_idx=2)
        q, xn = _norm_mod_proj(h, modtab, norm2_g[i], pr_w_q[i].astype(BF), shift_idx=3, scale_idx=4,
                               want_xn=True)
        h = _peer_residual(h.reshape(bsz * t, D), q.reshape(bsz * t, -1), xn.reshape(bsz * t, D),
                           pr_keys[i].astype(BF), pr_u[i].T.astype(BF), pr_v[i].astype(BF), modtab,
                           gate_idx=5, tok_per_batch=t).reshape(bsz, t, D)
    return _final_norm(h, final_g)
```

```python
import functools
import math

import jax
import jax.numpy as jnp
from jax import lax
from jax.experimental import pallas as pl
from jax.experimental.pallas import tpu as pltpu

D = 1024
CTX = 256
GRID_W = 64
EPS = 1e-6
ROPE_THETA = 10000.0
N_MOD = 6

ML_HEADS = 4
ML_DQK = 64
ML_DV = 128
DA_HEADS = 4
DA_DQK = 64
DA_DV = 128
WA_HEADS = 16
WA_KV_HEADS = 4
WA_GROUP = 4
WA_DH = 64
WINDOW = 128

PEER_HEADS = 8
PEER_NKEYS = 128
PEER_DHALF = 128
PEER_TOPK = 16

TM = 256
TQ_ML = 256
TQ_DA = 256
TQ_WA = 128
PEER_TN = 256
PEER_TE = 2048
PEER_CW = 256
GS_PITCH = 136
VMEM_LIMIT = 56 * 1024 * 1024

BF = jnp.bfloat16
F32 = jnp.float32
NEG_INF = float("-inf")


def _cparams(sem):
    return pltpu.CompilerParams(dimension_semantics=sem, vmem_limit_bytes=VMEM_LIMIT)


def _nt_dot(a, b):
    return lax.dot_general(a, b, (((1,), (1,)), ((), ())), preferred_element_type=F32)


def _dot(a, b):
    return jnp.dot(a, b, preferred_element_type=F32)


def _words(rows):
    return rows * jnp.dtype(BF).itemsize // 4


def _pack_rows(w):
    per = 4 // w.dtype.itemsize
    r, c = w.shape
    return lax.bitcast_convert_type(w.reshape(r // per, per, c).swapaxes(1, 2), jnp.uint32) if per > 1 else \
        lax.bitcast_convert_type(w, jnp.uint32)


def _mod_kernel(cc_ref, w_ref, b_ref, o_ref):
    a = jax.nn.silu(cc_ref[...]).astype(BF)
    o_ref[0] = _dot(a, w_ref[0].astype(BF)) + b_ref[0]


def _modulation(cc, mod_w, mod_b):
    depth = mod_w.shape[0]
    rows = cc.shape[0]
    tn = 1536
    return pl.pallas_call(
        _mod_kernel,
        grid=(depth, N_MOD * D // tn),
        in_specs=[pl.BlockSpec((rows, D), lambda l, n: (0, 0)),
                  pl.BlockSpec((1, D, tn), lambda l, n: (l, 0, n)),
                  pl.BlockSpec((1, 1, tn), lambda l, n: (l, 0, n))],
        out_specs=pl.BlockSpec((1, rows, tn), lambda l, n: (l, 0, n)),
        out_shape=jax.ShapeDtypeStruct((depth, rows, N_MOD * D), F32),
        compiler_params=_cparams(("parallel", "parallel")),
    )(cc, mod_w, mod_b.reshape(depth, 1, N_MOD * D))


def _mod_spec():
    return pl.BlockSpec((1, 1, N_MOD, D), lambda b, t: (b, jnp.minimum(t // (CTX // TM), 1), 0, 0))


def _proj_kernel(*refs, shift_idx, scale_idx, nrope, want_xn, split):
    it = iter(refs)
    h_ref, mod_ref, g_ref, wm_ref = next(it), next(it), next(it), next(it)
    if nrope:
        wr_ref, cos_ref, sin_ref = next(it), next(it), next(it)
    y_ref = next(it)
    x = h_ref[0]
    xn = x * lax.rsqrt(jnp.mean(x * x, axis=-1, keepdims=True) + EPS) * g_ref[...]
    mod = mod_ref[0, 0]
    xm = xn * (1.0 + mod[scale_idx:scale_idx + 1]) + mod[shift_idx:shift_idx + 1]
    xb = xm.astype(BF)
    acc = _dot(xb, wm_ref[...])
    if nrope:
        rot = _dot(xb, wr_ref[...])
        y_ref[0, :, :nrope] = acc[:, :nrope] * cos_ref[...] + rot * sin_ref[...]
        y_ref[0, :, nrope:] = acc[:, nrope:]
    elif split:
        for c in range(acc.shape[1] // split):
            y_ref[c] = acc[:, c * split:(c + 1) * split].astype(y_ref.dtype)
    else:
        y_ref[0] = acc
    if want_xn:
        next(it)[0] = pltpu.bitcast(xb, jnp.uint32)


def _norm_mod_proj(h, modtab, g, wm, wr=None, cos=None, sin=None, *, shift_idx, scale_idx, want_xn=False,
                   split=0):
    bsz, t, _ = h.shape
    nm = wm.shape[1]
    nrope = 0 if wr is None else wr.shape[1]
    in_specs = [pl.BlockSpec((1, TM, D), lambda b, i: (b, i, 0)),
                _mod_spec(),
                pl.BlockSpec((1, D), lambda b, i: (0, 0)),
                pl.BlockSpec((D, nm), lambda b, i: (0, 0))]
    args = [h, modtab, g.reshape(1, D), wm]
    if nrope:
        in_specs += [pl.BlockSpec((D, nrope), lambda b, i: (0, 0)),
                     pl.BlockSpec((TM, nrope), lambda b, i: (i, 0)),
                     pl.BlockSpec((TM, nrope), lambda b, i: (i, 0))]
        args += [wr, cos, sin]
    out_specs = [pl.BlockSpec((1, TM, nm), lambda b, i: (b, i, 0))]
    out_shape = [jax.ShapeDtypeStruct((bsz, t, nm), F32)]
    if split:
        out_specs = [pl.BlockSpec((nm // split, TM, split), lambda b, i: (0, b * (t // TM) + i, 0))]
        out_shape = [jax.ShapeDtypeStruct((nm // split, bsz * t, split), BF)]
    if want_xn:
        out_specs.append(pl.BlockSpec((1, _words(TM), D), lambda b, i: (b, i, 0)))
        out_shape.append(jax.ShapeDtypeStruct((bsz, _words(t), D), jnp.uint32))
    return pl.pallas_call(
        functools.partial(_proj_kernel, shift_idx=shift_idx, scale_idx=scale_idx, nrope=nrope, want_xn=want_xn,
                          split=split),
        grid=(bsz, t // TM), in_specs=in_specs, out_specs=out_specs, out_shape=out_shape,
        compiler_params=_cparams(("parallel", "parallel")),
    )(*args)


def _outproj_kernel(*refs, gate_idx, n_src):
    h_ref, mod_ref = refs[0], refs[1]
    o_ref = refs[2 + 2 * n_src]
    acc = None
    for s in range(n_src):
        part = _dot(refs[2 + 2 * s][0].astype(BF), refs[3 + 2 * s][...])
        acc = part if acc is None else acc + part
    gate = mod_ref[0, 0][gate_idx:gate_idx + 1]
    o_ref[0] = h_ref[0] + gate * acc


def _outproj_residual(h, modtab, srcs, *, gate_idx):
    bsz, t, _ = h.shape
    in_specs = [pl.BlockSpec((1, TM, D), lambda b, i: (b, i, 0)), _mod_spec()]
    args = [h, modtab]
    for x, w in srcs:
        k = x.shape[-1]
        in_specs += [pl.BlockSpec((1, TM, k), lambda b, i: (b, i, 0)),
                     pl.BlockSpec((k, D), lambda b, i: (0, 0))]
        args += [x, w]
    return pl.pallas_call(
        functools.partial(_outproj_kernel, gate_idx=gate_idx, n_src=len(srcs)),
        grid=(bsz, t // TM), in_specs=in_specs,
        out_specs=pl.BlockSpec((1, TM, D), lambda b, i: (b, i, 0)),
        out_shape=jax.ShapeDtypeStruct((bsz, t, D), F32),
        compiler_params=_cparams(("parallel", "parallel")),
    )(*args)


def _lane_scan(x, op, ident, lane, reverse):
    for k in range(7):
        s = 1 << k
        if reverse:
            sh = pltpu.roll(x, 128 - s, 1)
            x = op(x, jnp.where(lane < 128 - s, sh, ident))
        else:
            sh = pltpu.roll(x, s, 1)
            x = op(x, jnp.where(lane >= s, sh, ident))
    return x


def _gate_scan_kernel(g_ref, b_ref, gk_ref, qs_ref, *, t):
    nb = t // 128
    nh = ML_HEADS
    lane = lax.broadcasted_iota(jnp.int32, (nh, 128), 1)
    pre = g_ref[0] + b_ref[...]
    for d in range(2):
        ig = pre[(2 * d) * nh:(2 * d + 1) * nh]
        lf = jax.nn.log_sigmoid(pre[(2 * d + 1) * nh:(2 * d + 2) * nh])
        if d == 0:
            order = list(range(nb))
        else:
            order = list(range(CTX // 128 - 1, -1, -1)) + list(range(nb - 1, CTX // 128 - 1, -1))
        carry_b = jnp.zeros((nh, 1), F32)
        carry_m = jnp.full((nh, 1), NEG_INF, F32)
        edge = 0 if d else 127
        for blk in order:
            sl = slice(blk * 128, (blk + 1) * 128)
            bb = _lane_scan(lf[:, sl], jnp.add, 0.0, lane, bool(d)) + carry_b
            gg = ig[:, sl] - bb
            mm = jnp.maximum(_lane_scan(gg, jnp.maximum, NEG_INF, lane, bool(d)), carry_m)
            carry_b = bb[:, edge:edge + 1]
            carry_m = mm[:, edge:edge + 1]
            gk_ref[0, d * nh:(d + 1) * nh, sl] = gg
            qs_ref[0, (2 * d) * nh:(2 * d + 1) * nh, sl] = mm
            qs_ref[0, (2 * d + 1) * nh:(2 * d + 2) * nh, sl] = bb + mm


def _gate_scan(gates_t, gate_b):
    bsz, ng, t = gates_t.shape
    return pl.pallas_call(
        functools.partial(_gate_scan_kernel, t=t),
        grid=(bsz,),
        in_specs=[pl.BlockSpec((1, ng, t), lambda b: (b, 0, 0)),
                  pl.BlockSpec((ng, 1), lambda b: (0, 0))],
        out_specs=[pl.BlockSpec((1, 2 * ML_HEADS, t), lambda b: (b, 0, 0)),
                   pl.BlockSpec((1, 4 * ML_HEADS, t), lambda b: (b, 0, 0))],
        out_shape=[jax.ShapeDtypeStruct((bsz, 2 * ML_HEADS, t), F32),
                   jax.ShapeDtypeStruct((bsz, 4 * ML_HEADS, t), F32)],
        compiler_params=_cparams(("parallel",)),
    )(gates_t, gate_b.reshape(ng, 1))


def _mlstm_kernel(qk_ref, v_ref, o_ref, gk_ref, qs_ref, cw_ref, cb_ref, ng_ref, out_ref,
                  q_s, k_s, v_s, *, t):
    hd = pl.program_id(1)
    x = qk_ref[0]
    row = lax.broadcasted_iota(jnp.int32, (t, 1), 0)
    xp = jnp.where((row == 0) | (row == CTX), 0.0, pltpu.roll(x, 1, 0))
    xn = jnp.where((row == CTX - 1) | (row == t - 1), 0.0, pltpu.roll(x, t - 1, 0))
    cw = cw_ref[...]
    y = cb_ref[...] + xp * cw[0:1] + x * cw[1:2] + xn * cw[2:3]
    y = jax.nn.silu(y)
    q_s[...] = y[:, :ML_DQK].astype(BF)
    k_s[...] = (y[:, ML_DQK:] * (ML_DQK ** -0.5)).astype(BF)
    v_s[...] = v_ref[0].astype(BF)
    col = lax.broadcasted_iota(jnp.int32, (1, t), 1)
    col_b = jnp.where(col < CTX, CTX - 1 - col, t - 1 + CTX - col)

    def q_block(qi, carry):
        r0 = pl.multiple_of(qi * TQ_ML, TQ_ML)
        qb = q_s[pl.ds(r0, TQ_ML), :]
        s = _nt_dot(qb, k_s[...])
        rowa = r0 + lax.broadcasted_iota(jnp.int32, (TQ_ML, 1), 0)
        qs = qs_ref[0, 0, pl.ds(r0, TQ_ML), :]
        hsum = None
        for d in range(2):
            if d == 0:
                allowed = col <= rowa
            else:
                allowed = col_b <= jnp.where(rowa < CTX, CTX - 1 - rowa, t - 1 + CTX - rowa)
            gk = gk_ref[0, pl.ds(d * ML_HEADS + hd, 1), :]
            p = s * jnp.exp(jnp.where(allowed, gk - qs[:, 2 * d:2 * d + 1], NEG_INF))
            den = jnp.sum(p, axis=-1, keepdims=True)
            num = _dot(p.astype(BF), v_s[...])
            hdir = num / jnp.maximum(jnp.abs(den), jnp.exp(-qs[:, 2 * d + 1:2 * d + 2]))
            hsum = hdir if hsum is None else hsum + hdir
        yn = hsum * lax.rsqrt(jnp.mean(hsum * hsum, axis=-1, keepdims=True) + EPS) * ng_ref[...]
        out_ref[0, pl.ds(r0, TQ_ML), :] = yn * jax.nn.sigmoid(o_ref[0, pl.ds(r0, TQ_ML), :])
        return carry

    lax.fori_loop(0, t // TQ_ML, q_block, 0)


def _mlstm(y, gk, qs, conv_w, conv_b, norm_g, *, col_qk, col_v, col_o):
    bsz, t, _ = y.shape
    w = ML_HEADS * ML_DV
    return pl.pallas_call(
        functools.partial(_mlstm_kernel, t=t),
        grid=(bsz, ML_HEADS),
        in_specs=[pl.BlockSpec((1, t, 128), lambda b, h: (b, 0, col_qk // 128 + h)),
                  pl.BlockSpec((1, t, 128), lambda b, h: (b, 0, col_v // 128 + h)),
                  pl.BlockSpec((1, t, 128), lambda b, h: (b, 0, col_o // 128 + h)),
                  pl.BlockSpec((1, 2 * ML_HEADS, t), lambda b, h: (b, 0, 0)),
                  pl.BlockSpec((1, 1, t, 4), lambda b, h: (b, h, 0, 0)),
                  pl.BlockSpec((3, 128), lambda b, h: (0, h)),
                  pl.BlockSpec((1, 128), lambda b, h: (0, h)),
                  pl.BlockSpec((1, 128), lambda b, h: (0, h))],
        out_specs=pl.BlockSpec((1, t, 128), lambda b, h: (b, 0, h)),
        out_shape=jax.ShapeDtypeStruct((bsz, t, w), F32),
        scratch_shapes=[pltpu.VMEM((t, ML_DQK), BF), pltpu.VMEM((t, ML_DQK), BF), pltpu.VMEM((t, ML_DV), BF)],
        compiler_params=_cparams(("parallel", "parallel")),
    )(y, y, y, gk, qs, conv_w, conv_b.reshape(1, w), norm_g.reshape(1, w))


def _da_kernel(q_ref, k_ref, v_ref, lam_ref, ng_ref, out_ref, *, t, lam_init):
    qt = pl.program_id(2)
    lp = lam_ref[...]
    lam = (jnp.exp(jnp.sum(lp[0:1] * lp[1:2], axis=-1, keepdims=True))
           - jnp.exp(jnp.sum(lp[2:3] * lp[3:4], axis=-1, keepdims=True)) + lam_init)
    q = q_ref[0] * (DA_DQK ** -0.5)

    def attend(nk):
        v = v_ref[0, :nk, :].astype(BF)
        outs = []
        for m in range(2):
            qm = q[:, m * DA_DQK:(m + 1) * DA_DQK].astype(BF)
            km = k_ref[0, :nk, m * DA_DQK:(m + 1) * DA_DQK].astype(BF)
            s = _nt_dot(qm, km)
            e = jnp.exp(s - jnp.max(s, axis=-1, keepdims=True))
            outs.append(_dot(e.astype(BF), v) / jnp.sum(e, axis=-1, keepdims=True))
        o = outs[0] - lam * outs[1]
        yn = o * lax.rsqrt(jnp.mean(o * o, axis=-1, keepdims=True) + EPS) * ng_ref[...]
        out_ref[0] = yn * (1.0 - lam_init)

    @pl.when(qt < CTX // TQ_DA)
    def _():
        attend(CTX)

    @pl.when(qt >= CTX // TQ_DA)
    def _():
        attend(t)


def _diff_attention(y, da_lam, norm_g, *, col_q, col_k, col_v, lam_init):
    bsz, t, _ = y.shape
    w = DA_HEADS * DA_DV
    return pl.pallas_call(
        functools.partial(_da_kernel, t=t, lam_init=lam_init),
        grid=(bsz, DA_HEADS, t // TQ_DA),
        in_specs=[pl.BlockSpec((1, TQ_DA, 128), lambda b, h, i: (b, i, col_q // 128 + h)),
                  pl.BlockSpec((1, t, 128), lambda b, h, i: (b, 0, col_k // 128 + h)),
                  pl.BlockSpec((1, t, 128), lambda b, h, i: (b, 0, col_v // 128 + h)),
                  pl.BlockSpec((4, DA_DQK), lambda b, h, i: (0, 0)),
                  pl.BlockSpec((1, 128), lambda b, h, i: (0, h))],
        out_specs=pl.BlockSpec((1, TQ_DA, 128), lambda b, h, i: (b, i, h)),
        out_shape=jax.ShapeDtypeStruct((bsz, t, w), F32),
        compiler_params=_cparams(("parallel", "parallel", "parallel")),
    )(y, y, y, da_lam, norm_g.reshape(1, w))


def _wa_kernel(q_ref, k_ref, v_ref, sink_ref, out_ref, *, t):
    i = pl.program_id(1)
    n_ctx_blk = CTX // TQ_WA
    n_lat = t - CTX
    span = 3 * WINDOW

    def attend(local):
        if local:
            lat0 = jnp.clip((i - n_ctx_blk - 1) * TQ_WA, 0, n_lat - span)
            start = pl.multiple_of(CTX + lat0, TQ_WA)
            qpos = (i - n_ctx_blk) * TQ_WA + lax.broadcasted_iota(jnp.int32, (TQ_WA, 1), 0)
            kpos = lat0 + lax.broadcasted_iota(jnp.int32, (1, span), 1)
            mask = jnp.abs(qpos - kpos) <= WINDOW
        for g in range(WA_KV_HEADS):
            cs = slice(g * WA_DH, (g + 1) * WA_DH)
            kc = k_ref[0, :CTX, cs].astype(BF)
            vc = v_ref[0, :CTX, cs].astype(BF)
            if local:
                kl = k_ref[0, pl.ds(start, span), cs].astype(BF)
                vl = v_ref[0, pl.ds(start, span), cs].astype(BF)
            outs = []
            for hh in range(WA_GROUP):
                hd = g * WA_GROUP + hh
                q = (q_ref[0, :, hd * WA_DH:(hd + 1) * WA_DH] * (WA_DH ** -0.5)).astype(BF)
                sink = sink_ref[:, hd:hd + 1]
                sc = _nt_dot(q, kc)
                mx = jnp.maximum(jnp.max(sc, axis=-1, keepdims=True), sink)
                if local:
                    sl = jnp.where(mask, _nt_dot(q, kl), NEG_INF)
                    mx = jnp.maximum(mx, jnp.max(sl, axis=-1, keepdims=True))
                ec = jnp.exp(sc - mx)
                den = jnp.sum(ec, axis=-1, keepdims=True) + jnp.exp(sink - mx)
                num = _dot(ec.astype(BF), vc)
                if local:
                    el = jnp.exp(sl - mx)
                    den = den + jnp.sum(el, axis=-1, keepdims=True)
                    num = num + _dot(el.astype(BF), vl)
                outs.append(num / den)
            out_ref[0, :, g * WA_GROUP * WA_DH:(g + 1) * WA_GROUP * WA_DH] = jnp.concatenate(outs, axis=1)

    @pl.when(i < n_ctx_blk)
    def _():
        attend(False)

    @pl.when(i >= n_ctx_blk)
    def _():
        attend(True)


def _window_attention(y, sink, *, col_q, col_k, col_v):
    bsz, t, _ = y.shape
    qw = WA_HEADS * WA_DH
    kw = WA_KV_HEADS * WA_DH
    return pl.pallas_call(
        functools.partial(_wa_kernel, t=t),
        grid=(bsz, t // TQ_WA),
        in_specs=[pl.BlockSpec((1, TQ_WA, qw), lambda b, i: (b, i, col_q // qw)),
                  pl.BlockSpec((1, t, kw), lambda b, i: (b, 0, col_k // kw)),
                  pl.BlockSpec((1, t, kw), lambda b, i: (b, 0, col_v // kw)),
                  pl.BlockSpec((1, WA_HEADS), lambda b, i: (0, 0))],
        out_specs=pl.BlockSpec((1, TQ_WA, qw), lambda b, i: (b, i, 0)),
        out_shape=jax.ShapeDtypeStruct((bsz, t, qw), F32),
        compiler_params=_cparams(("parallel", "parallel")),
    )(y, y, y, sink.reshape(1, WA_HEADS))


def _pop_max(x, rowid):
    m = jnp.max(x, axis=0, keepdims=True)
    first = jnp.min(jnp.where(x == m, rowid, float(x.shape[0])), axis=0, keepdims=True)
    hit = rowid == first
    return m, first, hit, jnp.where(hit, NEG_INF, x)


_CAND = [(a, b) for a in range(PEER_TOPK) for b in range(PEER_TOPK) if (a + 1) * (b + 1) <= PEER_TOPK]


def _peer_kernel(q_ref, xn_ref, keys_ref, ut_ref, v_ref, h_ref, mod_ref, out_ref,
                 i1_s, i2_s, gt_s, i1t_s, i2t_s, gtt_s, gs_s, w_s, acc_s, *, gate_idx):
    r = pl.program_id(0)
    e = pl.program_id(1)
    tn = h_ref.shape[0]
    nslot = PEER_HEADS * PEER_TOPK

    @pl.when((e == 0) & (r == 0))
    def _init():
        i1_s[...] = jnp.zeros_like(i1_s)
        i2_s[...] = jnp.zeros_like(i2_s)
        gt_s[...] = jnp.zeros_like(gt_s)
        w_s[...] = jnp.zeros_like(w_s)
        acc_s[...] = jnp.zeros_like(acc_s)

    @pl.when(e == 0)
    def _gates():
        i1t_s[...] = i1_s[...].T
        i2t_s[...] = i2_s[...].T
        gtt_s[...] = gt_s[...].T
        sub = lax.broadcasted_iota(jnp.int32, (PEER_NKEYS, nslot), 0).astype(F32)

        def token(n, carry):
            at = jnp.where(sub == i1t_s[pl.ds(n, 1), :], gtt_s[pl.ds(n, 1), :], 0.0).astype(BF)
            bt = jnp.where(sub == i2t_s[pl.ds(n, 1), :], 1.0, 0.0).astype(BF)
            gs_s[pl.ds(pl.multiple_of(n * GS_PITCH, 8), PEER_NKEYS), :] = _nt_dot(at, bt)
            return carry

        lax.fori_loop(0, tn, token, 0, unroll=16)

    def route():
        rowid = lax.broadcasted_iota(jnp.int32, (PEER_NKEYS, tn), 0).astype(F32)
        crow = lax.broadcasted_iota(jnp.int32, (len(_CAND), tn), 0).astype(F32)
        tops, idxs = [], []
        for p in range(2):
            s = _nt_dot(keys_ref[0, p], q_ref[p])
            vals, ids = [], []
            for _ in range(PEER_TOPK):
                m, first, _, s = _pop_max(s, rowid)
                vals.append(m)
                ids.append(first)
                yield
            tops.append(vals)
            idxs.append(ids)
        work = jnp.concatenate([tops[0][a] + tops[1][b] for a, b in _CAND], axis=0)
        ci1 = jnp.concatenate([idxs[0][a] for a, _ in _CAND], axis=0)
        ci2 = jnp.concatenate([idxs[1][b] for _, b in _CAND], axis=0)
        sv, s1, s2 = [], [], []
        for _ in range(PEER_TOPK):
            m, _, hit, work = _pop_max(work, crow)
            sv.append(m)
            s1.append(jnp.max(jnp.where(hit, ci1, -1.0), axis=0, keepdims=True))
            s2.append(jnp.max(jnp.where(hit, ci2, -1.0), axis=0, keepdims=True))
            yield
        ev = jnp.exp(jnp.concatenate(sv, axis=0) - sv[0])
        rows = pl.ds(pl.multiple_of(e * PEER_TOPK, PEER_TOPK), PEER_TOPK)
        i1_s[rows, :] = jnp.concatenate(s1, axis=0)
        i2_s[rows, :] = jnp.concatenate(s2, axis=0)
        gt_s[rows, :] = ev * (0.5 / jnp.sum(ev, axis=0, keepdims=True))

    router = route()
    n_chunk = PEER_TE // PEER_CW
    pops_per_chunk = 3 * PEER_TOPK // n_chunk
    per_chunk = PEER_CW // PEER_NKEYS
    chunks_per_out = n_chunk * PEER_CW // D
    cur = e % 2
    restart = e == 1
    xn = pltpu.bitcast(xn_ref[...], BF)
    for c in range(n_chunk):
        if c % chunks_per_out == 0:
            cols = slice(c // chunks_per_out * PEER_CW, (c // chunks_per_out + 1) * PEER_CW)
            po = _dot(w_s[1 - cur], pltpu.bitcast(v_ref[:, cols], BF))
            acc_s[:, cols] = po + jnp.where(restart, 0.0, acc_s[:, cols])
        a = _dot(xn, pltpu.bitcast(ut_ref[:, c * PEER_CW:(c + 1) * PEER_CW], BF))
        for _ in range(pops_per_chunk):
            next(router, None)
        ws = []
        for k in range(per_chunk):
            i1 = e * (PEER_TE // PEER_NKEYS) + c * per_chunk + k
            g = gs_s[pl.ds(i1, tn, stride=GS_PITCH), :]
            ak = a[:, k * PEER_NKEYS:(k + 1) * PEER_NKEYS]
            ws.append((g * (ak * (1.0 + lax.erf(ak * (0.5 ** 0.5))))).astype(BF))
        w_s[cur, :, c * PEER_CW:(c + 1) * PEER_CW] = jnp.concatenate(ws, axis=1)
    for _ in router:
        pass

    @pl.when(e == 0)
    def _fin():
        gate = mod_ref[0, 0][gate_idx:gate_idx + 1]
        out_ref[...] = h_ref[...] + gate * acc_s[...]


def _peer_residual(h2, q2, xn2, keys, ut, v, modtab, *, gate_idx, tok_per_batch):
    ntok = h2.shape[0]
    nexp = ut.shape[1]
    nslot = PEER_HEADS * PEER_TOPK
    tn, te = PEER_TN, PEER_TE
    ntiles = ntok // tn
    tiles_per_batch = tok_per_batch // tn
    ctx_tiles = CTX // tn
    assert nexp // te == PEER_HEADS

    nsteps = nexp // te

    def lagged(r, lag):
        return jnp.clip(r - lag, 0, ntiles - 1)

    def mod_map(r, e):
        t = lagged(r, 2)
        return (t // tiles_per_batch, jnp.minimum((t % tiles_per_batch) // ctx_tiles, 1), 0, 0)

    return pl.pallas_call(
        functools.partial(_peer_kernel, gate_idx=gate_idx),
        grid=(ntiles + 2, nsteps),
        in_specs=[pl.BlockSpec((2, tn, PEER_DHALF), lambda r, e: (e, lagged(r, 0), 0)),
                  pl.BlockSpec((_words(tn), D), lambda r, e: (lagged(r, 1), 0)),
                  pl.BlockSpec((1, 2, PEER_NKEYS, PEER_DHALF), lambda r, e: (e, 0, 0, 0)),
                  pl.BlockSpec((_words(D), te), lambda r, e: (0, e)),
                  pl.BlockSpec((_words(te), D), lambda r, e: ((e + nsteps - 1) % nsteps, 0)),
                  pl.BlockSpec((tn, D), lambda r, e: (lagged(r, 2), 0)),
                  pl.BlockSpec((1, 1, N_MOD, D), mod_map)],
        out_specs=pl.BlockSpec((tn, D), lambda r, e: (lagged(r, 2), 0)),
        out_shape=jax.ShapeDtypeStruct((ntok, D), F32),
        scratch_shapes=[pltpu.VMEM((nslot, tn), F32),
                        pltpu.VMEM((nslot, tn), F32),
                        pltpu.VMEM((nslot, tn), F32),
                        pltpu.VMEM((tn, nslot), F32),
                        pltpu.VMEM((tn, nslot), F32),
                        pltpu.VMEM((tn, nslot), F32),
                        pltpu.VMEM((tn * GS_PITCH, PEER_NKEYS), F32),
                        pltpu.VMEM((2, tn, te), BF),
                        pltpu.VMEM((tn, D), F32)],
        compiler_params=_cparams(("arbitrary", "arbitrary")),
    )(q2, xn2, keys, ut, v, h2, modtab)


def _final_kernel(h_ref, g_ref, o_ref):
    x = h_ref[0]
    o_ref[0] = x * lax.rsqrt(jnp.mean(x * x, axis=-1, keepdims=True) + EPS) * g_ref[...]


def _final_norm(h, g):
    bsz, t, _ = h.shape
    s = t - CTX
    off = CTX // TM
    return pl.pallas_call(
        _final_kernel,
        grid=(bsz, s // TM),
        in_specs=[pl.BlockSpec((1, TM, D), lambda b, i: (b, i + off, 0)),
                  pl.BlockSpec((1, D), lambda b, i: (0, 0))],
        out_specs=pl.BlockSpec((1, TM, D), lambda b, i: (b, i, 0)),
        out_shape=jax.ShapeDtypeStruct((bsz, s, D), F32),
        compiler_params=_cparams(("parallel", "parallel")),
    )(h, g.reshape(1, D))


def _rot_half_cols(w, dh):
    d, n = w.shape
    wh = w.reshape(d, n // dh, 2, dh // 2)
    return jnp.concatenate([-wh[:, :, 1], wh[:, :, 0]], axis=2).reshape(d, n)


def _rope_tables(s, dh, width):
    rows = s // GRID_W
    r = jnp.repeat(jnp.arange(rows, dtype=F32), GRID_W)
    col = jnp.broadcast_to(jnp.arange(GRID_W, dtype=F32), (rows, GRID_W)).reshape(-1)
    nf = dh // 4
    inv = ROPE_THETA ** (-jnp.arange(nf, dtype=F32) / nf)
    ang = jnp.concatenate([r[:, None] * inv, col[:, None] * inv], axis=-1)
    cos = jnp.concatenate([jnp.ones((CTX, dh // 2), F32), jnp.cos(ang)], axis=0)
    sin = jnp.concatenate([jnp.zeros((CTX, dh // 2), F32), jnp.sin(ang)], axis=0)
    reps = width // (dh // 2)
    return jnp.tile(cos, (1, reps)), jnp.tile(sin, (1, reps))


def _interleave_heads(a, nh, dh):
    lead = a.shape[:-1]
    return a.reshape(lead + (2, nh, dh)).swapaxes(-3, -2).reshape(lead + (2 * nh * dh,))


def kernel(x, c, ctx, c_ctx, mod_w, mod_b, norm1_g, norm2_g, ev_w_in, ev_ml_conv_w, ev_ml_conv_b,
           ev_ml_gate_b, ev_ml_norm_g, ev_da_lam, ev_da_norm_g, ev_w_out, od_w_in, od_sink, od_w_out,
           pr_w_q, pr_keys, pr_u, pr_v, final_g):
    bsz, s, _ = x.shape
    depth = mod_w.shape[0]
    t = CTX + s
    ml_w = ML_HEADS * ML_DV
    da_w = DA_HEADS * DA_DV
    qk_w = 2 * ML_HEADS * ML_DQK
    da_qw = DA_HEADS * 2 * DA_DQK

    h = jnp.concatenate([ctx, x], axis=1)

    rows = -(-(bsz + 1) // 8) * 8
    cc = jnp.zeros((rows, D), F32).at[:bsz].set(c).at[bsz].set(c_ctx)
    mods = _modulation(cc, mod_w, mod_b)
    mod_lat = mods[:, :bsz].reshape(depth, bsz, N_MOD, D)
    mod_ctx = jnp.broadcast_to(mods[:, bsz].reshape(depth, 1, N_MOD, D), (depth, bsz, N_MOD, D))
    modtabs = jnp.stack([mod_ctx, mod_lat], axis=2)

    wa_rope_w = (WA_HEADS + WA_KV_HEADS) * WA_DH
    cos_t, sin_t = _rope_tables(s, WA_DH, wa_rope_w)

    for i in range(depth):
        j = i // 2
        modtab = modtabs[i]
        if i % 2 == 0:
            lam_init = 0.8 - 0.6 * math.exp(-0.3 * i)
            w = ev_w_in[j]
            o0 = 0
            w_mqk = w[:, o0:o0 + qk_w]; o0 += qk_w
            w_mv = w[:, o0:o0 + ml_w]; o0 += ml_w
            w_mo = w[:, o0:o0 + ml_w]; o0 += ml_w
            w_g = w[:, o0:o0 + 4 * ML_HEADS]; o0 += 4 * ML_HEADS
            w_dq = w[:, o0:o0 + da_qw]; o0 += da_qw
            w_dk = w[:, o0:o0 + da_qw]; o0 += da_qw
            w_dv = w[:, o0:o0 + da_w]
            w_rope = jnp.concatenate([w_dq, w_dk], axis=1)
            gpad = jnp.zeros((D, 128 - 4 * ML_HEADS), F32)
            wm = jnp.concatenate([w_rope, _interleave_heads(w_mqk, ML_HEADS, ML_DQK), w_mv, w_mo, w_dv, w_g, gpad],
                                 axis=1).astype(BF)
            wr = _rot_half_cols(w_rope, DA_DQK).astype(BF)
            nr = 2 * da_qw
            col_mqk = nr
            col_mv = col_mqk + qk_w
            col_mo = col_mv + ml_w
            col_dv = col_mo + ml_w
            col_g = col_dv + da_w
            (y,) = _norm_mod_proj(h, modtab, norm1_g[i], wm, wr, cos_t[:, :nr], sin_t[:, :nr],
                                  shift_idx=0, scale_idx=1)
            gates_t = jnp.swapaxes(y[:, :, col_g:col_g + 4 * ML_HEADS], 1, 2)
            gk, qs = _gate_scan(gates_t, ev_ml_gate_b[j])
            qs = qs.reshape(bsz, 4, ML_HEADS, t).transpose(0, 2, 3, 1)
            ml = _mlstm(y, gk, qs, _interleave_heads(ev_ml_conv_w[j], ML_HEADS, ML_DQK),
                        _interleave_heads(ev_ml_conv_b[j], ML_HEADS, ML_DQK), ev_ml_norm_g[j],
                        col_qk=col_mqk, col_v=col_mv, col_o=col_mo)
            da = _diff_attention(y, ev_da_lam[j], ev_da_norm_g[j], col_q=0, col_k=da_qw, col_v=col_dv,
                                 lam_init=lam_init)
            w_out = ev_w_out[j].astype(BF)
            h = _outproj_residual(h, modtab, [(ml, w_out[:ml_w]), (da, w_out[ml_w:])], gate_idx=2)
        else:
            w = od_w_in[j]
            wm = w.astype(BF)
            wr = _rot_half_cols(w[:, :wa_rope_w], WA_DH).astype(BF)
            (y,) = _norm_mod_proj(h, modtab, norm1_g[i], wm, wr, cos_t, sin_t, shift_idx=0, scale_idx=1)
            att = _window_attention(y, od_sink[j], col_q=0, col_k=WA_HEADS * WA_DH,
                                    col_v=WA_HEADS * WA_DH + WA_KV_HEADS * WA_DH)
            h = _outproj_residual(h, modtab, [(att, od_w_out[j].astype(BF))], gate_idx=2)
        q, xn = _norm_mod_proj(h, modtab, norm2_g[i], pr_w_q[i].astype(BF), shift_idx=3, scale_idx=4,
                               want_xn=True, split=PEER_DHALF)
        h = _peer_residual(h.reshape(bsz * t, D), q, xn.reshape(-1, D), pr_keys[i].astype(BF),
                           _pack_rows(pr_u[i].T.astype(BF)), _pack_rows(pr_v[i].astype(BF)), modtab,
                           gate_idx=5, tok_per_batch=t).reshape(bsz, t, D)
    return _final_norm(h, final_g)
```

```python
import functools
import math

import jax
import jax.numpy as jnp
from jax import lax
from jax.experimental import pallas as pl
from jax.experimental.pallas import tpu as pltpu

D = 1024
CTX = 256
GRID_W = 64
EPS = 1e-6
ROPE_THETA = 10000.0
N_MOD = 6

ML_HEADS = 4
ML_DQK = 64
ML_DV = 128
DA_HEADS = 4
DA_DQK = 64
DA_DV = 128
WA_HEADS = 16
WA_KV_HEADS = 4
WA_GROUP = 4
WA_DH = 64
WINDOW = 128

PEER_HEADS = 8
PEER_NKEYS = 128
PEER_DHALF = 128
PEER_TOPK = 16

TM = 256
TQ_ML = 256
TQ_DA = 256
TQ_WA = 128
PEER_TN = 256
PEER_TE = 2048
PEER_CW = 256
GS_PITCH = 136
VMEM_LIMIT = 56 * 1024 * 1024

BF = jnp.bfloat16
F32 = jnp.float32
NEG_INF = float("-inf")


def _cparams(sem):
    return pltpu.CompilerParams(dimension_semantics=sem, vmem_limit_bytes=VMEM_LIMIT)


def _nt_dot(a, b):
    return lax.dot_general(a, b, (((1,), (1,)), ((), ())), preferred_element_type=F32)


def _tn_dot(a, b):
    return lax.dot_general(a, b, (((0,), (0,)), ((), ())), preferred_element_type=F32)


def _dot(a, b):
    return jnp.dot(a, b, preferred_element_type=F32)


def _words(rows):
    return rows * jnp.dtype(BF).itemsize // 4


def _mod_kernel(cc_ref, w_ref, b_ref, o_ref):
    a = jax.nn.silu(cc_ref[...]).astype(BF)
    o_ref[0] = _dot(a, w_ref[0].astype(BF)) + b_ref[0]


def _modulation(cc, mod_w, mod_b):
    depth = mod_w.shape[0]
    rows = cc.shape[0]
    tn = 1536
    return pl.pallas_call(
        _mod_kernel,
        grid=(depth, N_MOD * D // tn),
        in_specs=[pl.BlockSpec((rows, D), lambda l, n: (0, 0)),
                  pl.BlockSpec((1, D, tn), lambda l, n: (l, 0, n)),
                  pl.BlockSpec((1, 1, tn), lambda l, n: (l, 0, n))],
        out_specs=pl.BlockSpec((1, rows, tn), lambda l, n: (l, 0, n)),
        out_shape=jax.ShapeDtypeStruct((depth, rows, N_MOD * D), F32),
        compiler_params=_cparams(("parallel", "parallel")),
    )(cc, mod_w, mod_b.reshape(depth, 1, N_MOD * D))


def _mod_spec():
    return pl.BlockSpec((1, 1, N_MOD, D), lambda b, t: (b, jnp.minimum(t // (CTX // TM), 1), 0, 0))


def _proj_kernel(*refs, shift_idx, scale_idx, nrope, want_xn, split):
    it = iter(refs)
    h_ref, mod_ref, g_ref, wm_ref = next(it), next(it), next(it), next(it)
    if nrope:
        wr_ref, cos_ref, sin_ref = next(it), next(it), next(it)
    y_ref = next(it)
    x = h_ref[0]
    xn = x * lax.rsqrt(jnp.mean(x * x, axis=-1, keepdims=True) + EPS) * g_ref[...]
    mod = mod_ref[0, 0]
    xm = xn * (1.0 + mod[scale_idx:scale_idx + 1]) + mod[shift_idx:shift_idx + 1]
    xb = xm.astype(BF)
    acc = _dot(xb, wm_ref[...])
    if nrope:
        rot = _dot(xb, wr_ref[...])
        y_ref[0, :, :nrope] = acc[:, :nrope] * cos_ref[...] + rot * sin_ref[...]
        y_ref[0, :, nrope:] = acc[:, nrope:]
    elif split:
        for c in range(acc.shape[1] // split):
            y_ref[c] = acc[:, c * split:(c + 1) * split].astype(y_ref.dtype)
    else:
        y_ref[0] = acc
    if want_xn:
        next(it)[0] = pltpu.bitcast(xb, jnp.uint32)


def _norm_mod_proj(h, modtab, g, wm, wr=None, cos=None, sin=None, *, shift_idx, scale_idx, want_xn=False,
                   split=0):
    bsz, t, _ = h.shape
    nm = wm.shape[1]
    nrope = 0 if wr is None else wr.shape[1]
    in_specs = [pl.BlockSpec((1, TM, D), lambda b, i: (b, i, 0)),
                _mod_spec(),
                pl.BlockSpec((1, D), lambda b, i: (0, 0)),
                pl.BlockSpec((D, nm), lambda b, i: (0, 0))]
    args = [h, modtab, g.reshape(1, D), wm]
    if nrope:
        in_specs += [pl.BlockSpec((D, nrope), lambda b, i: (0, 0)),
                     pl.BlockSpec((TM, nrope), lambda b, i: (i, 0)),
                     pl.BlockSpec((TM, nrope), lambda b, i: (i, 0))]
        args += [wr, cos, sin]
    out_specs = [pl.BlockSpec((1, TM, nm), lambda b, i: (b, i, 0))]
    out_shape = [jax.ShapeDtypeStruct((bsz, t, nm), F32)]
    if split:
        out_specs = [pl.BlockSpec((nm // split, TM, split), lambda b, i: (0, b * (t // TM) + i, 0))]
        out_shape = [jax.ShapeDtypeStruct((nm // split, bsz * t, split), BF)]
    if want_xn:
        out_specs.append(pl.BlockSpec((1, _words(TM), D), lambda b, i: (b, i, 0)))
        out_shape.append(jax.ShapeDtypeStruct((bsz, _words(t), D), jnp.uint32))
    return pl.pallas_call(
        functools.partial(_proj_kernel, shift_idx=shift_idx, scale_idx=scale_idx, nrope=nrope, want_xn=want_xn,
                          split=split),
        grid=(bsz, t // TM), in_specs=in_specs, out_specs=out_specs, out_shape=out_shape,
        compiler_params=_cparams(("parallel", "parallel")),
    )(*args)


def _outproj_kernel(*refs, gate_idx, n_src):
    h_ref, mod_ref = refs[0], refs[1]
    o_ref = refs[2 + 2 * n_src]
    acc = None
    for s in range(n_src):
        part = _dot(refs[2 + 2 * s][0].astype(BF), refs[3 + 2 * s][...])
        acc = part if acc is None else acc + part
    gate = mod_ref[0, 0][gate_idx:gate_idx + 1]
    o_ref[0] = h_ref[0] + gate * acc


def _outproj_residual(h, modtab, srcs, *, gate_idx):
    bsz, t, _ = h.shape
    in_specs = [pl.BlockSpec((1, TM, D), lambda b, i: (b, i, 0)), _mod_spec()]
    args = [h, modtab]
    for x, w in srcs:
        k = x.shape[-1]
        in_specs += [pl.BlockSpec((1, TM, k), lambda b, i: (b, i, 0)),
                     pl.BlockSpec((k, D), lambda b, i: (0, 0))]
        args += [x, w]
    return pl.pallas_call(
        functools.partial(_outproj_kernel, gate_idx=gate_idx, n_src=len(srcs)),
        grid=(bsz, t // TM), in_specs=in_specs,
        out_specs=pl.BlockSpec((1, TM, D), lambda b, i: (b, i, 0)),
        out_shape=jax.ShapeDtypeStruct((bsz, t, D), F32),
        compiler_params=_cparams(("parallel", "parallel")),
    )(*args)


def _lane_scan(x, op, ident, lane, reverse):
    for k in range(7):
        s = 1 << k
        if reverse:
            sh = pltpu.roll(x, 128 - s, 1)
            x = op(x, jnp.where(lane < 128 - s, sh, ident))
        else:
            sh = pltpu.roll(x, s, 1)
            x = op(x, jnp.where(lane >= s, sh, ident))
    return x


def _gate_scan_kernel(g_ref, b_ref, gk_ref, qs_ref, *, t):
    nb = t // 128
    nh = ML_HEADS
    lane = lax.broadcasted_iota(jnp.int32, (nh, 128), 1)
    pre = g_ref[0] + b_ref[...]
    for d in range(2):
        ig = pre[(2 * d) * nh:(2 * d + 1) * nh]
        lf = jax.nn.log_sigmoid(pre[(2 * d + 1) * nh:(2 * d + 2) * nh])
        if d == 0:
            order = list(range(nb))
        else:
            order = list(range(CTX // 128 - 1, -1, -1)) + list(range(nb - 1, CTX // 128 - 1, -1))
        carry_b = jnp.zeros((nh, 1), F32)
        carry_m = jnp.full((nh, 1), NEG_INF, F32)
        edge = 0 if d else 127
        for blk in order:
            sl = slice(blk * 128, (blk + 1) * 128)
            bb = _lane_scan(lf[:, sl], jnp.add, 0.0, lane, bool(d)) + carry_b
            gg = ig[:, sl] - bb
            mm = jnp.maximum(_lane_scan(gg, jnp.maximum, NEG_INF, lane, bool(d)), carry_m)
            carry_b = bb[:, edge:edge + 1]
            carry_m = mm[:, edge:edge + 1]
            gk_ref[0, d * nh:(d + 1) * nh, sl] = gg
            qs_ref[0, (2 * d) * nh:(2 * d + 1) * nh, sl] = mm
            qs_ref[0, (2 * d + 1) * nh:(2 * d + 2) * nh, sl] = bb + mm


def _gate_scan(gates_t, gate_b):
    bsz, ng, t = gates_t.shape
    return pl.pallas_call(
        functools.partial(_gate_scan_kernel, t=t),
        grid=(bsz,),
        in_specs=[pl.BlockSpec((1, ng, t), lambda b: (b, 0, 0)),
                  pl.BlockSpec((ng, 1), lambda b: (0, 0))],
        out_specs=[pl.BlockSpec((1, 2 * ML_HEADS, t), lambda b: (b, 0, 0)),
                   pl.BlockSpec((1, 4 * ML_HEADS, t), lambda b: (b, 0, 0))],
        out_shape=[jax.ShapeDtypeStruct((bsz, 2 * ML_HEADS, t), F32),
                   jax.ShapeDtypeStruct((bsz, 4 * ML_HEADS, t), F32)],
        compiler_params=_cparams(("parallel",)),
    )(gates_t, gate_b.reshape(ng, 1))


def _mlstm_kernel(qk_ref, v_ref, o_ref, gk_ref, qs_ref, cw_ref, cb_ref, ng_ref, out_ref,
                  q_s, k_s, v_s, *, t):
    hd = pl.program_id(1)
    x = qk_ref[0]
    row = lax.broadcasted_iota(jnp.int32, (t, 1), 0)
    xp = jnp.where((row == 0) | (row == CTX), 0.0, pltpu.roll(x, 1, 0))
    xn = jnp.where((row == CTX - 1) | (row == t - 1), 0.0, pltpu.roll(x, t - 1, 0))
    cw = cw_ref[...]
    y = cb_ref[...] + xp * cw[0:1] + x * cw[1:2] + xn * cw[2:3]
    y = jax.nn.silu(y)
    q_s[...] = y[:, :ML_DQK].astype(BF)
    k_s[...] = (y[:, ML_DQK:] * (ML_DQK ** -0.5)).astype(BF)
    v_s[...] = v_ref[0].astype(BF)
    lower = lax.broadcasted_iota(jnp.int32, (1, TQ_ML), 1) <= lax.broadcasted_iota(jnp.int32, (TQ_ML, 1), 0)
    upper = lax.broadcasted_iota(jnp.int32, (1, TQ_ML), 1) >= lax.broadcasted_iota(jnp.int32, (TQ_ML, 1), 0)

    for qi in range(t // TQ_ML):
        r0 = qi * TQ_ML
        rows = slice(r0, r0 + TQ_ML)
        s = _nt_dot(q_s[rows, :], k_s[...])
        qs = qs_ref[0, 0, rows, :]
        fwd = ([(0, r0, None)] if r0 else []) + [(r0, r0 + TQ_ML, lower)]
        seg_end = CTX if r0 < CTX else t
        bwd = ([(0, CTX, None)] if r0 >= CTX else []) + [(r0, r0 + TQ_ML, upper)]
        bwd += [(r0 + TQ_ML, seg_end, None)] if r0 + TQ_ML < seg_end else []
        hsum = None
        for d, pieces in enumerate((fwd, bwd)):
            gk = gk_ref[0, pl.ds(d * ML_HEADS + hd, 1), :]
            den = num = None
            for a, b, tri in pieces:
                z = gk[:, a:b] - qs[:, 2 * d:2 * d + 1]
                p = s[:, a:b] * jnp.exp(z if tri is None else jnp.where(tri, z, NEG_INF))
                dp = jnp.sum(p, axis=-1, keepdims=True)
                np_ = _dot(p.astype(BF), v_s[a:b, :])
                den = dp if den is None else den + dp
                num = np_ if num is None else num + np_
            hdir = num / jnp.maximum(jnp.abs(den), jnp.exp(-qs[:, 2 * d + 1:2 * d + 2]))
            hsum = hdir if hsum is None else hsum + hdir
        yn = hsum * lax.rsqrt(jnp.mean(hsum * hsum, axis=-1, keepdims=True) + EPS) * ng_ref[...]
        out_ref[0, rows, :] = yn * jax.nn.sigmoid(o_ref[0, rows, :])


def _mlstm(y, gk, qs, conv_w, conv_b, norm_g, *, col_qk, col_v, col_o):
    bsz, t, _ = y.shape
    w = ML_HEADS * ML_DV
    return pl.pallas_call(
        functools.partial(_mlstm_kernel, t=t),
        grid=(bsz, ML_HEADS),
        in_specs=[pl.BlockSpec((1, t, 128), lambda b, h: (b, 0, col_qk // 128 + h)),
                  pl.BlockSpec((1, t, 128), lambda b, h: (b, 0, col_v // 128 + h)),
                  pl.BlockSpec((1, t, 128), lambda b, h: (b, 0, col_o // 128 + h)),
                  pl.BlockSpec((1, 2 * ML_HEADS, t), lambda b, h: (b, 0, 0)),
                  pl.BlockSpec((1, 1, t, 4), lambda b, h: (b, h, 0, 0)),
                  pl.BlockSpec((3, 128), lambda b, h: (0, h)),
                  pl.BlockSpec((1, 128), lambda b, h: (0, h)),
                  pl.BlockSpec((1, 128), lambda b, h: (0, h))],
        out_specs=pl.BlockSpec((1, t, 128), lambda b, h: (b, 0, h)),
        out_shape=jax.ShapeDtypeStruct((bsz, t, w), F32),
        scratch_shapes=[pltpu.VMEM((t, ML_DQK), BF), pltpu.VMEM((t, ML_DQK), BF), pltpu.VMEM((t, ML_DV), BF)],
        compiler_params=_cparams(("parallel", "parallel")),
    )(y, y, y, gk, qs, conv_w, conv_b.reshape(1, w), norm_g.reshape(1, w))


def _da_kernel(q_ref, k_ref, v_ref, lam_ref, ng_ref, out_ref, *, t, lam_init):
    qt = pl.program_id(2)
    lp = lam_ref[...]
    lam = (jnp.exp(jnp.sum(lp[0:1] * lp[1:2], axis=-1, keepdims=True))
           - jnp.exp(jnp.sum(lp[2:3] * lp[3:4], axis=-1, keepdims=True)) + lam_init)
    q = q_ref[0] * (DA_DQK ** -0.5)

    def attend(nk):
        v = v_ref[0, :nk, :].astype(BF)
        outs = []
        for m in range(2):
            qm = q[:, m * DA_DQK:(m + 1) * DA_DQK].astype(BF)
            km = k_ref[0, :nk, m * DA_DQK:(m + 1) * DA_DQK].astype(BF)
            s = _nt_dot(qm, km)
            e = jnp.exp(s - jnp.max(s, axis=-1, keepdims=True))
            outs.append(_dot(e.astype(BF), v) / jnp.sum(e, axis=-1, keepdims=True))
        o = outs[0] - lam * outs[1]
        yn = o * lax.rsqrt(jnp.mean(o * o, axis=-1, keepdims=True) + EPS) * ng_ref[...]
        out_ref[0] = yn * (1.0 - lam_init)

    @pl.when(qt < CTX // TQ_DA)
    def _():
        attend(CTX)

    @pl.when(qt >= CTX // TQ_DA)
    def _():
        attend(t)


def _diff_attention(y, da_lam, norm_g, *, col_q, col_k, col_v, lam_init):
    bsz, t, _ = y.shape
    w = DA_HEADS * DA_DV
    return pl.pallas_call(
        functools.partial(_da_kernel, t=t, lam_init=lam_init),
        grid=(bsz, DA_HEADS, t // TQ_DA),
        in_specs=[pl.BlockSpec((1, TQ_DA, 128), lambda b, h, i: (b, i, col_q // 128 + h)),
                  pl.BlockSpec((1, t, 128), lambda b, h, i: (b, 0, col_k // 128 + h)),
                  pl.BlockSpec((1, t, 128), lambda b, h, i: (b, 0, col_v // 128 + h)),
                  pl.BlockSpec((4, DA_DQK), lambda b, h, i: (0, 0)),
                  pl.BlockSpec((1, 128), lambda b, h, i: (0, h))],
        out_specs=pl.BlockSpec((1, TQ_DA, 128), lambda b, h, i: (b, i, h)),
        out_shape=jax.ShapeDtypeStruct((bsz, t, w), F32),
        compiler_params=_cparams(("parallel", "parallel", "parallel")),
    )(y, y, y, da_lam, norm_g.reshape(1, w))


def _wa_kernel(q_ref, k_ref, v_ref, sink_ref, out_ref, *, t):
    i = pl.program_id(1)
    n_ctx_blk = CTX // TQ_WA
    n_lat = t - CTX
    span = 3 * WINDOW

    def attend(local):
        if local:
            lat0 = jnp.clip((i - n_ctx_blk - 1) * TQ_WA, 0, n_lat - span)
            start = pl.multiple_of(CTX + lat0, TQ_WA)
            qcol = lax.broadcasted_iota(jnp.int32, (1, WA_GROUP, TQ_WA), 2).reshape(1, WA_GROUP * TQ_WA)
            kpos = lat0 + lax.broadcasted_iota(jnp.int32, (span, 1), 0)
            mask = jnp.abs((i - n_ctx_blk) * TQ_WA + qcol - kpos) <= WINDOW
        for g in range(WA_KV_HEADS):
            cs = slice(g * WA_DH, (g + 1) * WA_DH)
            heads = range(g * WA_GROUP, (g + 1) * WA_GROUP)
            kc = k_ref[0, :CTX, cs].astype(BF)
            vc = v_ref[0, :CTX, cs].astype(BF)
            q = jnp.concatenate([q_ref[0, :, hd * WA_DH:(hd + 1) * WA_DH] for hd in heads], axis=0)
            q = (q * (WA_DH ** -0.5)).astype(BF)
            sink = jnp.concatenate([jnp.broadcast_to(sink_ref[:, hd:hd + 1], (1, TQ_WA)) for hd in heads], axis=1)
            sc = _nt_dot(kc, q)
            mx = jnp.maximum(jnp.max(sc, axis=0, keepdims=True), sink)
            if local:
                kl = k_ref[0, pl.ds(start, span), cs].astype(BF)
                vl = v_ref[0, pl.ds(start, span), cs].astype(BF)
                sl = jnp.where(mask, _nt_dot(kl, q), NEG_INF)
                mx = jnp.maximum(mx, jnp.max(sl, axis=0, keepdims=True))
            ec = jnp.exp(sc - mx)
            den = jnp.sum(ec, axis=0, keepdims=True) + jnp.exp(sink - mx)
            num = _tn_dot(vc, ec.astype(BF))
            if local:
                el = jnp.exp(sl - mx)
                den = den + jnp.sum(el, axis=0, keepdims=True)
                num = num + _tn_dot(vl, el.astype(BF))
            o = (num / den).T
            out_ref[0, :, g * WA_GROUP * WA_DH:(g + 1) * WA_GROUP * WA_DH] = jnp.concatenate(
                [o[hh * TQ_WA:(hh + 1) * TQ_WA] for hh in range(WA_GROUP)], axis=1)

    @pl.when(i < n_ctx_blk)
    def _():
        attend(False)

    @pl.when(i >= n_ctx_blk)
    def _():
        attend(True)


def _window_attention(y, sink, *, col_q, col_k, col_v):
    bsz, t, _ = y.shape
    qw = WA_HEADS * WA_DH
    kw = WA_KV_HEADS * WA_DH
    return pl.pallas_call(
        functools.partial(_wa_kernel, t=t),
        grid=(bsz, t // TQ_WA),
        in_specs=[pl.BlockSpec((1, TQ_WA, qw), lambda b, i: (b, i, col_q // qw)),
                  pl.BlockSpec((1, t, kw), lambda b, i: (b, 0, col_k // kw)),
                  pl.BlockSpec((1, t, kw), lambda b, i: (b, 0, col_v // kw)),
                  pl.BlockSpec((1, WA_HEADS), lambda b, i: (0, 0))],
        out_specs=pl.BlockSpec((1, TQ_WA, qw), lambda b, i: (b, i, 0)),
        out_shape=jax.ShapeDtypeStruct((bsz, t, qw), F32),
        compiler_params=_cparams(("parallel", "parallel")),
    )(y, y, y, sink.reshape(1, WA_HEADS))


def _pop_max(x, rowid):
    m = jnp.max(x, axis=0, keepdims=True)
    first = jnp.min(jnp.where(x == m, rowid, float(x.shape[0])), axis=0, keepdims=True)
    hit = rowid == first
    return m, first, hit, jnp.where(hit, NEG_INF, x)


_CAND = [(a, b) for a in range(PEER_TOPK) for b in range(PEER_TOPK) if (a + 1) * (b + 1) <= PEER_TOPK]


def _peer_kernel(q_ref, xn_ref, keys_ref, ut_ref, v_ref, h_ref, mod_ref, out_ref,
                 i1_s, i2_s, gt_s, i1t_s, i2t_s, gtt_s, gs_s, w_s, acc_s, *, gate_idx):
    r = pl.program_id(0)
    e = pl.program_id(1)
    tn = h_ref.shape[0]
    nslot = PEER_HEADS * PEER_TOPK

    @pl.when((e == 0) & (r == 0))
    def _init():
        i1_s[...] = jnp.zeros_like(i1_s)
        i2_s[...] = jnp.zeros_like(i2_s)
        gt_s[...] = jnp.zeros_like(gt_s)
        w_s[...] = jnp.zeros_like(w_s)
        acc_s[...] = jnp.zeros_like(acc_s)

    @pl.when(e == 0)
    def _gates():
        i1t_s[...] = i1_s[...].T
        i2t_s[...] = i2_s[...].T
        gtt_s[...] = gt_s[...].T
        sub = lax.broadcasted_iota(jnp.int32, (PEER_NKEYS, nslot), 0).astype(F32)

        def token(n, carry):
            at = jnp.where(sub == i1t_s[pl.ds(n, 1), :], gtt_s[pl.ds(n, 1), :], 0.0).astype(BF)
            bt = jnp.where(sub == i2t_s[pl.ds(n, 1), :], 1.0, 0.0).astype(BF)
            gs_s[pl.ds(pl.multiple_of(n * GS_PITCH, 8), PEER_NKEYS), :] = _nt_dot(at, bt)
            return carry

        lax.fori_loop(0, tn, token, 0, unroll=16)

    def route():
        rowid = lax.broadcasted_iota(jnp.int32, (PEER_NKEYS, tn), 0).astype(F32)
        crow = lax.broadcasted_iota(jnp.int32, (len(_CAND), tn), 0).astype(F32)
        tops, idxs = [], []
        for p in range(2):
            s = _nt_dot(keys_ref[0, p], q_ref[p])
            vals, ids = [], []
            for _ in range(PEER_TOPK):
                m, first, _, s = _pop_max(s, rowid)
                vals.append(m)
                ids.append(first)
                yield
            tops.append(vals)
            idxs.append(ids)
        work = jnp.concatenate([tops[0][a] + tops[1][b] for a, b in _CAND], axis=0)
        ci1 = jnp.concatenate([idxs[0][a] for a, _ in _CAND], axis=0)
        ci2 = jnp.concatenate([idxs[1][b] for _, b in _CAND], axis=0)
        sv, s1, s2 = [], [], []
        for _ in range(PEER_TOPK):
            m, _, hit, work = _pop_max(work, crow)
            sv.append(m)
            s1.append(jnp.max(jnp.where(hit, ci1, -1.0), axis=0, keepdims=True))
            s2.append(jnp.max(jnp.where(hit, ci2, -1.0), axis=0, keepdims=True))
            yield
        ev = jnp.exp(jnp.concatenate(sv, axis=0) - sv[0])
        rows = pl.ds(pl.multiple_of(e * PEER_TOPK, PEER_TOPK), PEER_TOPK)
        i1_s[rows, :] = jnp.concatenate(s1, axis=0)
        i2_s[rows, :] = jnp.concatenate(s2, axis=0)
        gt_s[rows, :] = ev * (0.5 / jnp.sum(ev, axis=0, keepdims=True))

    router = route()
    n_chunk = PEER_TE // PEER_CW
    pops_per_chunk = 3 * PEER_TOPK // n_chunk
    per_chunk = PEER_CW // PEER_NKEYS
    chunks_per_out = n_chunk * PEER_CW // D
    cur = e % 2
    restart = e == 1
    xn = pltpu.bitcast(xn_ref[...], BF)
    for c in range(n_chunk):
        if c % chunks_per_out == 0:
            cols = slice(c // chunks_per_out * PEER_CW, (c // chunks_per_out + 1) * PEER_CW)
            po = _dot(w_s[1 - cur], pltpu.bitcast(v_ref[:, cols], BF))
            acc_s[:, cols] = po + jnp.where(restart, 0.0, acc_s[:, cols])
        a = _dot(xn, pltpu.bitcast(ut_ref[:, c * PEER_CW:(c + 1) * PEER_CW], BF))
        for _ in range(pops_per_chunk):
            next(router, None)
        ws = []
        for k in range(per_chunk):
            i1 = e * (PEER_TE // PEER_NKEYS) + c * per_chunk + k
            g = gs_s[pl.ds(i1, tn, stride=GS_PITCH), :]
            ak = a[:, k * PEER_NKEYS:(k + 1) * PEER_NKEYS]
            ws.append((g * (ak * (1.0 + lax.erf(ak * (0.5 ** 0.5))))).astype(BF))
        w_s[cur, :, c * PEER_CW:(c + 1) * PEER_CW] = jnp.concatenate(ws, axis=1)
    for _ in router:
        pass

    @pl.when(e == 0)
    def _fin():
        gate = mod_ref[0, 0][gate_idx:gate_idx + 1]
        out_ref[...] = h_ref[...] + gate * acc_s[...]


def _peer_residual(h2, q2, xn2, keys, ut, v, modtab, *, gate_idx, tok_per_batch):
    ntok = h2.shape[0]
    nexp = ut.shape[1]
    nslot = PEER_HEADS * PEER_TOPK
    tn, te = PEER_TN, PEER_TE
    ntiles = ntok // tn
    tiles_per_batch = tok_per_batch // tn
    ctx_tiles = CTX // tn
    assert nexp // te == PEER_HEADS

    nsteps = nexp // te

    def lagged(r, lag):
        return jnp.clip(r - lag, 0, ntiles - 1)

    def mod_map(r, e):
        t = lagged(r, 2)
        return (t // tiles_per_batch, jnp.minimum((t % tiles_per_batch) // ctx_tiles, 1), 0, 0)

    return pl.pallas_call(
        functools.partial(_peer_kernel, gate_idx=gate_idx),
        grid=(ntiles + 2, nsteps),
        in_specs=[pl.BlockSpec((2, tn, PEER_DHALF), lambda r, e: (e, lagged(r, 0), 0)),
                  pl.BlockSpec((_words(tn), D), lambda r, e: (lagged(r, 1), 0)),
                  pl.BlockSpec((1, 2, PEER_NKEYS, PEER_DHALF), lambda r, e: (e, 0, 0, 0)),
                  pl.BlockSpec((_words(D), te), lambda r, e: (0, e)),
                  pl.BlockSpec((_words(te), D), lambda r, e: ((e + nsteps - 1) % nsteps, 0)),
                  pl.BlockSpec((tn, D), lambda r, e: (lagged(r, 2), 0)),
                  pl.BlockSpec((1, 1, N_MOD, D), mod_map)],
        out_specs=pl.BlockSpec((tn, D), lambda r, e: (lagged(r, 2), 0)),
        out_shape=jax.ShapeDtypeStruct((ntok, D), F32),
        scratch_shapes=[pltpu.VMEM((nslot, tn), F32),
                        pltpu.VMEM((nslot, tn), F32),
                        pltpu.VMEM((nslot, tn), F32),
                        pltpu.VMEM((tn, nslot), F32),
                        pltpu.VMEM((tn, nslot), F32),
                        pltpu.VMEM((tn, nslot), F32),
                        pltpu.VMEM((tn * GS_PITCH, PEER_NKEYS), F32),
                        pltpu.VMEM((2, tn, te), BF),
                        pltpu.VMEM((tn, D), F32)],
        compiler_params=_cparams(("arbitrary", "arbitrary")),
    )(q2, xn2, keys, ut, v, h2, modtab)


def _pack_kernel(w_ref, o_ref, *, transpose):
    w = w_ref[0].T if transpose else w_ref[0]
    o_ref[0] = pltpu.bitcast(w.astype(BF), jnp.uint32)


def _pack_experts(w, *, transpose):
    nl, ne, d = w.shape
    te = 512
    if transpose:
        out_block, out_map, out_dims = (1, _words(d), te), (lambda l, i: (l, 0, i)), (nl, _words(d), ne)
    else:
        out_block, out_map, out_dims = (1, _words(te), d), (lambda l, i: (l, i, 0)), (nl, _words(ne), d)
    return pl.pallas_call(
        functools.partial(_pack_kernel, transpose=transpose),
        grid=(nl, ne // te),
        in_specs=[pl.BlockSpec((1, te, d), lambda l, i: (l, i, 0))],
        out_specs=pl.BlockSpec(out_block, out_map),
        out_shape=jax.ShapeDtypeStruct(out_dims, jnp.uint32),
        compiler_params=_cparams(("parallel", "parallel")),
    )(w)


def _final_kernel(h_ref, g_ref, o_ref):
    x = h_ref[0]
    o_ref[0] = x * lax.rsqrt(jnp.mean(x * x, axis=-1, keepdims=True) + EPS) * g_ref[...]


def _final_norm(h, g):
    bsz, t, _ = h.shape
    s = t - CTX
    off = CTX // TM
    return pl.pallas_call(
        _final_kernel,
        grid=(bsz, s // TM),
        in_specs=[pl.BlockSpec((1, TM, D), lambda b, i: (b, i + off, 0)),
                  pl.BlockSpec((1, D), lambda b, i: (0, 0))],
        out_specs=pl.BlockSpec((1, TM, D), lambda b, i: (b, i, 0)),
        out_shape=jax.ShapeDtypeStruct((bsz, s, D), F32),
        compiler_params=_cparams(("parallel", "parallel")),
    )(h, g.reshape(1, D))


def _rot_half_cols(w, dh):
    d, n = w.shape
    wh = w.reshape(d, n // dh, 2, dh // 2)
    return jnp.concatenate([-wh[:, :, 1], wh[:, :, 0]], axis=2).reshape(d, n)


def _rope_tables(s, dh, width):
    rows = s // GRID_W
    r = jnp.repeat(jnp.arange(rows, dtype=F32), GRID_W)
    col = jnp.broadcast_to(jnp.arange(GRID_W, dtype=F32), (rows, GRID_W)).reshape(-1)
    nf = dh // 4
    inv = ROPE_THETA ** (-jnp.arange(nf, dtype=F32) / nf)
    ang = jnp.concatenate([r[:, None] * inv, col[:, None] * inv], axis=-1)
    cos = jnp.concatenate([jnp.ones((CTX, dh // 2), F32), jnp.cos(ang)], axis=0)
    sin = jnp.concatenate([jnp.zeros((CTX, dh // 2), F32), jnp.sin(ang)], axis=0)
    reps = width // (dh // 2)
    return jnp.tile(cos, (1, reps)), jnp.tile(sin, (1, reps))


def _interleave_heads(a, nh, dh):
    lead = a.shape[:-1]
    return a.reshape(lead + (2, nh, dh)).swapaxes(-3, -2).reshape(lead + (2 * nh * dh,))


def kernel(x, c, ctx, c_ctx, mod_w, mod_b, norm1_g, norm2_g, ev_w_in, ev_ml_conv_w, ev_ml_conv_b,
           ev_ml_gate_b, ev_ml_norm_g, ev_da_lam, ev_da_norm_g, ev_w_out, od_w_in, od_sink, od_w_out,
           pr_w_q, pr_keys, pr_u, pr_v, final_g):
    bsz, s, _ = x.shape
    depth = mod_w.shape[0]
    t = CTX + s
    ml_w = ML_HEADS * ML_DV
    da_w = DA_HEADS * DA_DV
    qk_w = 2 * ML_HEADS * ML_DQK
    da_qw = DA_HEADS * 2 * DA_DQK

    h = jnp.concatenate([ctx, x], axis=1)

    rows = -(-(bsz + 1) // 8) * 8
    cc = jnp.zeros((rows, D), F32).at[:bsz].set(c).at[bsz].set(c_ctx)
    mods = _modulation(cc, mod_w, mod_b)
    mod_lat = mods[:, :bsz].reshape(depth, bsz, N_MOD, D)
    mod_ctx = jnp.broadcast_to(mods[:, bsz].reshape(depth, 1, N_MOD, D), (depth, bsz, N_MOD, D))
    modtabs = jnp.stack([mod_ctx, mod_lat], axis=2)

    wa_rope_w = (WA_HEADS + WA_KV_HEADS) * WA_DH
    cos_t, sin_t = _rope_tables(s, WA_DH, wa_rope_w)
    ut_words = _pack_experts(pr_u, transpose=True)
    v_words = _pack_experts(pr_v, transpose=False)

    for i in range(depth):
        j = i // 2
        modtab = modtabs[i]
        if i % 2 == 0:
            lam_init = 0.8 - 0.6 * math.exp(-0.3 * i)
            w = ev_w_in[j]
            o0 = 0
            w_mqk = w[:, o0:o0 + qk_w]; o0 += qk_w
            w_mv = w[:, o0:o0 + ml_w]; o0 += ml_w
            w_mo = w[:, o0:o0 + ml_w]; o0 += ml_w
            w_g = w[:, o0:o0 + 4 * ML_HEADS]; o0 += 4 * ML_HEADS
            w_dq = w[:, o0:o0 + da_qw]; o0 += da_qw
            w_dk = w[:, o0:o0 + da_qw]; o0 += da_qw
            w_dv = w[:, o0:o0 + da_w]
            w_rope = jnp.concatenate([w_dq, w_dk], axis=1)
            gpad = jnp.zeros((D, 128 - 4 * ML_HEADS), F32)
            wm = jnp.concatenate([w_rope, _interleave_heads(w_mqk, ML_HEADS, ML_DQK), w_mv, w_mo, w_dv, w_g, gpad],
                                 axis=1).astype(BF)
            wr = _rot_half_cols(w_rope, DA_DQK).astype(BF)
            nr = 2 * da_qw
            col_mqk = nr
            col_mv = col_mqk + qk_w
            col_mo = col_mv + ml_w
            col_dv = col_mo + ml_w
            col_g = col_dv + da_w
            (y,) = _norm_mod_proj(h, modtab, norm1_g[i], wm, wr, cos_t[:, :nr], sin_t[:, :nr],
                                  shift_idx=0, scale_idx=1)
            gates_t = jnp.swapaxes(y[:, :, col_g:col_g + 4 * ML_HEADS], 1, 2)
            gk, qs = _gate_scan(gates_t, ev_ml_gate_b[j])
            qs = qs.reshape(bsz, 4, ML_HEADS, t).transpose(0, 2, 3, 1)
            ml = _mlstm(y, gk, qs, _interleave_heads(ev_ml_conv_w[j], ML_HEADS, ML_DQK),
                        _interleave_heads(ev_ml_conv_b[j], ML_HEADS, ML_DQK), ev_ml_norm_g[j],
                        col_qk=col_mqk, col_v=col_mv, col_o=col_mo)
            da = _diff_attention(y, ev_da_lam[j], ev_da_norm_g[j], col_q=0, col_k=da_qw, col_v=col_dv,
                                 lam_init=lam_init)
            w_out = ev_w_out[j].astype(BF)
            h = _outproj_residual(h, modtab, [(ml, w_out[:ml_w]), (da, w_out[ml_w:])], gate_idx=2)
        else:
            w = od_w_in[j]
            wm = w.astype(BF)
            wr = _rot_half_cols(w[:, :wa_rope_w], WA_DH).astype(BF)
            (y,) = _norm_mod_proj(h, modtab, norm1_g[i], wm, wr, cos_t, sin_t, shift_idx=0, scale_idx=1)
            att = _window_attention(y, od_sink[j], col_q=0, col_k=WA_HEADS * WA_DH,
                                    col_v=WA_HEADS * WA_DH + WA_KV_HEADS * WA_DH)
            h = _outproj_residual(h, modtab, [(att, od_w_out[j].astype(BF))], gate_idx=2)
        q, xn = _norm_mod_proj(h, modtab, norm2_g[i], pr_w_q[i].astype(BF), shift_idx=3, scale_idx=4,
                               want_xn=True, split=PEER_DHALF)
        h = _peer_residual(h.reshape(bsz * t, D), q, xn.reshape(-1, D), pr_keys[i].astype(BF),
                           ut_words[i], v_words[i], modtab,
                           gate_idx=5, tok_per_batch=t).reshape(bsz, t, D)
    return _final_norm(h, final_g)
```

```python
import functools
import math

import jax
import jax.numpy as jnp
from jax import lax
from jax.experimental import pallas as pl
from jax.experimental.pallas import tpu as pltpu

D = 1024
CTX = 256
GRID_W = 64
EPS = 1e-6
ROPE_THETA = 10000.0
N_MOD = 6

ML_HEADS = 4
ML_DQK = 64
ML_DV = 128
DA_HEADS = 4
DA_DQK = 64
DA_DV = 128
WA_HEADS = 16
WA_KV_HEADS = 4
WA_GROUP = 4
WA_DH = 64
WINDOW = 128

PEER_HEADS = 8
PEER_NKEYS = 128
PEER_DHALF = 128
PEER_TOPK = 16

TM = 256
TQ_ML = 256
TQ_DA = 256
TQ_WA = 128
PEER_TN = 256
PEER_TE = 2048
PEER_CW = 256
GS_PITCH = 136
VMEM_LIMIT = 56 * 1024 * 1024

BF = jnp.bfloat16
F32 = jnp.float32
NEG_INF = float("-inf")


def _cparams(sem):
    return pltpu.CompilerParams(dimension_semantics=sem, vmem_limit_bytes=VMEM_LIMIT)


def _nt_dot(a, b):
    return lax.dot_general(a, b, (((1,), (1,)), ((), ())), preferred_element_type=F32)


def _tn_dot(a, b):
    return lax.dot_general(a, b, (((0,), (0,)), ((), ())), preferred_element_type=F32)


def _dot(a, b):
    return jnp.dot(a, b, preferred_element_type=F32)


def _words(rows):
    return rows * jnp.dtype(BF).itemsize // 4


def _mod_kernel(cc_ref, w_ref, b_ref, o_ref):
    a = jax.nn.silu(cc_ref[...]).astype(BF)
    o_ref[0] = _dot(a, w_ref[0].astype(BF)) + b_ref[0]


def _modulation(cc, mod_w, mod_b):
    depth = mod_w.shape[0]
    rows = cc.shape[0]
    tn = 1536
    return pl.pallas_call(
        _mod_kernel,
        grid=(depth, N_MOD * D // tn),
        in_specs=[pl.BlockSpec((rows, D), lambda l, n: (0, 0)),
                  pl.BlockSpec((1, D, tn), lambda l, n: (l, 0, n)),
                  pl.BlockSpec((1, 1, tn), lambda l, n: (l, 0, n))],
        out_specs=pl.BlockSpec((1, rows, tn), lambda l, n: (l, 0, n)),
        out_shape=jax.ShapeDtypeStruct((depth, rows, N_MOD * D), F32),
        compiler_params=_cparams(("parallel", "parallel")),
    )(cc, mod_w, mod_b.reshape(depth, 1, N_MOD * D))


def _is_latent_tile(tile, ctx_tiles):
    return jnp.minimum(tile // ctx_tiles, 1) if ctx_tiles else 1


def _mod_spec(ctx_rows=CTX):
    return pl.BlockSpec((1, 1, N_MOD, D), lambda b, t: (b, _is_latent_tile(t, ctx_rows // TM), 0, 0))


def _proj_kernel(*refs, shift_idx, scale_idx, nrope, want_xn, split):
    it = iter(refs)
    h_ref, mod_ref, g_ref, wm_ref = next(it), next(it), next(it), next(it)
    if nrope:
        wr_ref, cos_ref, sin_ref = next(it), next(it), next(it)
    y_ref = next(it)
    x = h_ref[0]
    xn = x * lax.rsqrt(jnp.mean(x * x, axis=-1, keepdims=True) + EPS) * g_ref[...]
    mod = mod_ref[0, 0]
    xm = xn * (1.0 + mod[scale_idx:scale_idx + 1]) + mod[shift_idx:shift_idx + 1]
    xb = xm.astype(BF)
    acc = _dot(xb, wm_ref[...])
    if nrope:
        rot = _dot(xb, wr_ref[...])
        y_ref[0, :, :nrope] = acc[:, :nrope] * cos_ref[...] + rot * sin_ref[...]
        y_ref[0, :, nrope:] = acc[:, nrope:]
    elif split:
        for c in range(acc.shape[1] // split):
            y_ref[c] = acc[:, c * split:(c + 1) * split].astype(y_ref.dtype)
    else:
        y_ref[0] = acc
    if want_xn:
        next(it)[0] = pltpu.bitcast(xb, jnp.uint32)


def _norm_mod_proj(h, modtab, g, wm, wr=None, cos=None, sin=None, *, shift_idx, scale_idx, want_xn=False,
                   split=0, ctx_rows=CTX):
    bsz, t, _ = h.shape
    nm = wm.shape[1]
    nrope = 0 if wr is None else wr.shape[1]
    in_specs = [pl.BlockSpec((1, TM, D), lambda b, i: (b, i, 0)),
                _mod_spec(ctx_rows),
                pl.BlockSpec((1, D), lambda b, i: (0, 0)),
                pl.BlockSpec((D, nm), lambda b, i: (0, 0))]
    args = [h, modtab, g.reshape(1, D), wm]
    if nrope:
        in_specs += [pl.BlockSpec((D, nrope), lambda b, i: (0, 0)),
                     pl.BlockSpec((TM, nrope), lambda b, i: (i, 0)),
                     pl.BlockSpec((TM, nrope), lambda b, i: (i, 0))]
        args += [wr, cos, sin]
    out_specs = [pl.BlockSpec((1, TM, nm), lambda b, i: (b, i, 0))]
    out_shape = [jax.ShapeDtypeStruct((bsz, t, nm), F32)]
    if split:
        out_specs = [pl.BlockSpec((nm // split, TM, split), lambda b, i: (0, b * (t // TM) + i, 0))]
        out_shape = [jax.ShapeDtypeStruct((nm // split, bsz * t, split), BF)]
    if want_xn:
        out_specs.append(pl.BlockSpec((1, _words(TM), D), lambda b, i: (b, i, 0)))
        out_shape.append(jax.ShapeDtypeStruct((bsz, _words(t), D), jnp.uint32))
    return pl.pallas_call(
        functools.partial(_proj_kernel, shift_idx=shift_idx, scale_idx=scale_idx, nrope=nrope, want_xn=want_xn,
                          split=split),
        grid=(bsz, t // TM), in_specs=in_specs, out_specs=out_specs, out_shape=out_shape,
        compiler_params=_cparams(("parallel", "parallel")),
    )(*args)


def _outproj_kernel(*refs, gate_idx, n_src):
    h_ref, mod_ref = refs[0], refs[1]
    o_ref = refs[2 + 2 * n_src]
    acc = None
    for s in range(n_src):
        part = _dot(refs[2 + 2 * s][0].astype(BF), refs[3 + 2 * s][...])
        acc = part if acc is None else acc + part
    gate = mod_ref[0, 0][gate_idx:gate_idx + 1]
    o_ref[0] = h_ref[0] + gate * acc


def _outproj_residual(h, modtab, srcs, *, gate_idx):
    bsz, t, _ = h.shape
    in_specs = [pl.BlockSpec((1, TM, D), lambda b, i: (b, i, 0)), _mod_spec()]
    args = [h, modtab]
    for x, w in srcs:
        k = x.shape[-1]
        in_specs += [pl.BlockSpec((1, TM, k), lambda b, i: (b, i, 0)),
                     pl.BlockSpec((k, D), lambda b, i: (0, 0))]
        args += [x, w]
    return pl.pallas_call(
        functools.partial(_outproj_kernel, gate_idx=gate_idx, n_src=len(srcs)),
        grid=(bsz, t // TM), in_specs=in_specs,
        out_specs=pl.BlockSpec((1, TM, D), lambda b, i: (b, i, 0)),
        out_shape=jax.ShapeDtypeStruct((bsz, t, D), F32),
        compiler_params=_cparams(("parallel", "parallel")),
    )(*args)


def _lane_scan(x, op, ident, lane, reverse):
    for k in range(7):
        s = 1 << k
        if reverse:
            sh = pltpu.roll(x, 128 - s, 1)
            x = op(x, jnp.where(lane < 128 - s, sh, ident))
        else:
            sh = pltpu.roll(x, s, 1)
            x = op(x, jnp.where(lane >= s, sh, ident))
    return x


def _gate_scan_kernel(g_ref, b_ref, gk_ref, qs_ref, *, t):
    nb = t // 128
    nh = ML_HEADS
    lane = lax.broadcasted_iota(jnp.int32, (nh, 128), 1)
    pre = g_ref[0] + b_ref[...]
    for d in range(2):
        ig = pre[(2 * d) * nh:(2 * d + 1) * nh]
        lf = jax.nn.log_sigmoid(pre[(2 * d + 1) * nh:(2 * d + 2) * nh])
        if d == 0:
            order = list(range(nb))
        else:
            order = list(range(CTX // 128 - 1, -1, -1)) + list(range(nb - 1, CTX // 128 - 1, -1))
        carry_b = jnp.zeros((nh, 1), F32)
        carry_m = jnp.full((nh, 1), NEG_INF, F32)
        edge = 0 if d else 127
        for blk in order:
            sl = slice(blk * 128, (blk + 1) * 128)
            bb = _lane_scan(lf[:, sl], jnp.add, 0.0, lane, bool(d)) + carry_b
            gg = ig[:, sl] - bb
            mm = jnp.maximum(_lane_scan(gg, jnp.maximum, NEG_INF, lane, bool(d)), carry_m)
            carry_b = bb[:, edge:edge + 1]
            carry_m = mm[:, edge:edge + 1]
            gk_ref[0, d * nh:(d + 1) * nh, sl] = gg
            qs_ref[0, (2 * d) * nh:(2 * d + 1) * nh, sl] = mm
            qs_ref[0, (2 * d + 1) * nh:(2 * d + 2) * nh, sl] = bb + mm


def _gate_scan(gates_t, gate_b):
    bsz, ng, t = gates_t.shape
    return pl.pallas_call(
        functools.partial(_gate_scan_kernel, t=t),
        grid=(bsz,),
        in_specs=[pl.BlockSpec((1, ng, t), lambda b: (b, 0, 0)),
                  pl.BlockSpec((ng, 1), lambda b: (0, 0))],
        out_specs=[pl.BlockSpec((1, 2 * ML_HEADS, t), lambda b: (b, 0, 0)),
                   pl.BlockSpec((1, 4 * ML_HEADS, t), lambda b: (b, 0, 0))],
        out_shape=[jax.ShapeDtypeStruct((bsz, 2 * ML_HEADS, t), F32),
                   jax.ShapeDtypeStruct((bsz, 4 * ML_HEADS, t), F32)],
        compiler_params=_cparams(("parallel",)),
    )(gates_t, gate_b.reshape(ng, 1))


def _mlstm_kernel(qk_ref, v_ref, o_ref, gk_ref, qs_ref, cw_ref, cb_ref, ng_ref, out_ref,
                  q_s, k_s, v_s, *, t):
    hd = pl.program_id(1)
    x = qk_ref[0]
    row = lax.broadcasted_iota(jnp.int32, (t, 1), 0)
    xp = jnp.where((row == 0) | (row == CTX), 0.0, pltpu.roll(x, 1, 0))
    xn = jnp.where((row == CTX - 1) | (row == t - 1), 0.0, pltpu.roll(x, t - 1, 0))
    cw = cw_ref[...]
    y = cb_ref[...] + xp * cw[0:1] + x * cw[1:2] + xn * cw[2:3]
    y = jax.nn.silu(y)
    q_s[...] = y[:, :ML_DQK].astype(BF)
    k_s[...] = (y[:, ML_DQK:] * (ML_DQK ** -0.5)).astype(BF)
    v_s[...] = v_ref[0].astype(BF)
    lower = lax.broadcasted_iota(jnp.int32, (1, TQ_ML), 1) <= lax.broadcasted_iota(jnp.int32, (TQ_ML, 1), 0)
    upper = lax.broadcasted_iota(jnp.int32, (1, TQ_ML), 1) >= lax.broadcasted_iota(jnp.int32, (TQ_ML, 1), 0)

    for qi in range(t // TQ_ML):
        r0 = qi * TQ_ML
        rows = slice(r0, r0 + TQ_ML)
        s = _nt_dot(q_s[rows, :], k_s[...])
        qs = qs_ref[0, 0, rows, :]
        fwd = ([(0, r0, None)] if r0 else []) + [(r0, r0 + TQ_ML, lower)]
        seg_end = CTX if r0 < CTX else t
        bwd = ([(0, CTX, None)] if r0 >= CTX else []) + [(r0, r0 + TQ_ML, upper)]
        bwd += [(r0 + TQ_ML, seg_end, None)] if r0 + TQ_ML < seg_end else []
        hsum = None
        for d, pieces in enumerate((fwd, bwd)):
            gk = gk_ref[0, pl.ds(d * ML_HEADS + hd, 1), :]
            den = num = None
            for a, b, tri in pieces:
                z = gk[:, a:b] - qs[:, 2 * d:2 * d + 1]
                p = s[:, a:b] * jnp.exp(z if tri is None else jnp.where(tri, z, NEG_INF))
                dp = jnp.sum(p, axis=-1, keepdims=True)
                np_ = _dot(p.astype(BF), v_s[a:b, :])
                den = dp if den is None else den + dp
                num = np_ if num is None else num + np_
            hdir = num / jnp.maximum(jnp.abs(den), jnp.exp(-qs[:, 2 * d + 1:2 * d + 2]))
            hsum = hdir if hsum is None else hsum + hdir
        yn = hsum * lax.rsqrt(jnp.mean(hsum * hsum, axis=-1, keepdims=True) + EPS) * ng_ref[...]
        out_ref[0, rows, :] = yn * jax.nn.sigmoid(o_ref[0, rows, :])


def _mlstm(y, gk, qs, conv_w, conv_b, norm_g, *, col_qk, col_v, col_o):
    bsz, t, _ = y.shape
    w = ML_HEADS * ML_DV
    return pl.pallas_call(
        functools.partial(_mlstm_kernel, t=t),
        grid=(bsz, ML_HEADS),
        in_specs=[pl.BlockSpec((1, t, 128), lambda b, h: (b, 0, col_qk // 128 + h)),
                  pl.BlockSpec((1, t, 128), lambda b, h: (b, 0, col_v // 128 + h)),
                  pl.BlockSpec((1, t, 128), lambda b, h: (b, 0, col_o // 128 + h)),
                  pl.BlockSpec((1, 2 * ML_HEADS, t), lambda b, h: (b, 0, 0)),
                  pl.BlockSpec((1, 1, t, 4), lambda b, h: (b, h, 0, 0)),
                  pl.BlockSpec((3, 128), lambda b, h: (0, h)),
                  pl.BlockSpec((1, 128), lambda b, h: (0, h)),
                  pl.BlockSpec((1, 128), lambda b, h: (0, h))],
        out_specs=pl.BlockSpec((1, t, 128), lambda b, h: (b, 0, h)),
        out_shape=jax.ShapeDtypeStruct((bsz, t, w), F32),
        scratch_shapes=[pltpu.VMEM((t, ML_DQK), BF), pltpu.VMEM((t, ML_DQK), BF), pltpu.VMEM((t, ML_DV), BF)],
        compiler_params=_cparams(("parallel", "parallel")),
    )(y, y, y, gk, qs, conv_w, conv_b.reshape(1, w), norm_g.reshape(1, w))


def _da_kernel(q_ref, k_ref, v_ref, lam_ref, ng_ref, out_ref, *, t, lam_init):
    qt = pl.program_id(2)
    lp = lam_ref[...]
    lam = (jnp.exp(jnp.sum(lp[0:1] * lp[1:2], axis=-1, keepdims=True))
           - jnp.exp(jnp.sum(lp[2:3] * lp[3:4], axis=-1, keepdims=True)) + lam_init)
    q = q_ref[0] * (DA_DQK ** -0.5)

    def attend(nk):
        v = v_ref[0, :nk, :].astype(BF)
        outs = []
        for m in range(2):
            qm = q[:, m * DA_DQK:(m + 1) * DA_DQK].astype(BF)
            km = k_ref[0, :nk, m * DA_DQK:(m + 1) * DA_DQK].astype(BF)
            s = _nt_dot(qm, km)
            e = jnp.exp(s - jnp.max(s, axis=-1, keepdims=True))
            outs.append(_dot(e.astype(BF), v) / jnp.sum(e, axis=-1, keepdims=True))
        o = outs[0] - lam * outs[1]
        yn = o * lax.rsqrt(jnp.mean(o * o, axis=-1, keepdims=True) + EPS) * ng_ref[...]
        out_ref[0] = yn * (1.0 - lam_init)

    @pl.when(qt < CTX // TQ_DA)
    def _():
        attend(CTX)

    @pl.when(qt >= CTX // TQ_DA)
    def _():
        attend(t)


def _diff_attention(y, da_lam, norm_g, *, col_q, col_k, col_v, lam_init):
    bsz, t, _ = y.shape
    w = DA_HEADS * DA_DV
    return pl.pallas_call(
        functools.partial(_da_kernel, t=t, lam_init=lam_init),
        grid=(bsz, DA_HEADS, t // TQ_DA),
        in_specs=[pl.BlockSpec((1, TQ_DA, 128), lambda b, h, i: (b, i, col_q // 128 + h)),
                  pl.BlockSpec((1, t, 128), lambda b, h, i: (b, 0, col_k // 128 + h)),
                  pl.BlockSpec((1, t, 128), lambda b, h, i: (b, 0, col_v // 128 + h)),
                  pl.BlockSpec((4, DA_DQK), lambda b, h, i: (0, 0)),
                  pl.BlockSpec((1, 128), lambda b, h, i: (0, h))],
        out_specs=pl.BlockSpec((1, TQ_DA, 128), lambda b, h, i: (b, i, h)),
        out_shape=jax.ShapeDtypeStruct((bsz, t, w), F32),
        compiler_params=_cparams(("parallel", "parallel", "parallel")),
    )(y, y, y, da_lam, norm_g.reshape(1, w))


def _wa_kernel(q_ref, k_ref, v_ref, sink_ref, out_ref, *, t):
    i = pl.program_id(1)
    n_ctx_blk = CTX // TQ_WA
    n_lat = t - CTX
    span = 3 * WINDOW

    def attend(local):
        if local:
            lat0 = jnp.clip((i - n_ctx_blk - 1) * TQ_WA, 0, n_lat - span)
            start = pl.multiple_of(CTX + lat0, TQ_WA)
            qcol = lax.broadcasted_iota(jnp.int32, (1, WA_GROUP, TQ_WA), 2).reshape(1, WA_GROUP * TQ_WA)
            kpos = lat0 + lax.broadcasted_iota(jnp.int32, (span, 1), 0)
            mask = jnp.abs((i - n_ctx_blk) * TQ_WA + qcol - kpos) <= WINDOW
        for g in range(WA_KV_HEADS):
            cs = slice(g * WA_DH, (g + 1) * WA_DH)
            heads = range(g * WA_GROUP, (g + 1) * WA_GROUP)
            kc = k_ref[0, :CTX, cs].astype(BF)
            vc = v_ref[0, :CTX, cs].astype(BF)
            q = jnp.concatenate([q_ref[0, :, hd * WA_DH:(hd + 1) * WA_DH] for hd in heads], axis=0)
            q = (q * (WA_DH ** -0.5)).astype(BF)
            sink = jnp.concatenate([jnp.broadcast_to(sink_ref[:, hd:hd + 1], (1, TQ_WA)) for hd in heads], axis=1)
            sc = _nt_dot(kc, q)
            mx = jnp.maximum(jnp.max(sc, axis=0, keepdims=True), sink)
            if local:
                kl = k_ref[0, pl.ds(start, span), cs].astype(BF)
                vl = v_ref[0, pl.ds(start, span), cs].astype(BF)
                sl = jnp.where(mask, _nt_dot(kl, q), NEG_INF)
                mx = jnp.maximum(mx, jnp.max(sl, axis=0, keepdims=True))
            ec = jnp.exp(sc - mx)
            den = jnp.sum(ec, axis=0, keepdims=True) + jnp.exp(sink - mx)
            num = _tn_dot(vc, ec.astype(BF))
            if local:
                el = jnp.exp(sl - mx)
                den = den + jnp.sum(el, axis=0, keepdims=True)
                num = num + _tn_dot(vl, el.astype(BF))
            o = (num / den).T
            out_ref[0, :, g * WA_GROUP * WA_DH:(g + 1) * WA_GROUP * WA_DH] = jnp.concatenate(
                [o[hh * TQ_WA:(hh + 1) * TQ_WA] for hh in range(WA_GROUP)], axis=1)

    @pl.when(i < n_ctx_blk)
    def _():
        attend(False)

    @pl.when(i >= n_ctx_blk)
    def _():
        attend(True)


def _window_attention(y, sink, *, col_q, col_k, col_v):
    bsz, t, _ = y.shape
    qw = WA_HEADS * WA_DH
    kw = WA_KV_HEADS * WA_DH
    return pl.pallas_call(
        functools.partial(_wa_kernel, t=t),
        grid=(bsz, t // TQ_WA),
        in_specs=[pl.BlockSpec((1, TQ_WA, qw), lambda b, i: (b, i, col_q // qw)),
                  pl.BlockSpec((1, t, kw), lambda b, i: (b, 0, col_k // kw)),
                  pl.BlockSpec((1, t, kw), lambda b, i: (b, 0, col_v // kw)),
                  pl.BlockSpec((1, WA_HEADS), lambda b, i: (0, 0))],
        out_specs=pl.BlockSpec((1, TQ_WA, qw), lambda b, i: (b, i, 0)),
        out_shape=jax.ShapeDtypeStruct((bsz, t, qw), F32),
        compiler_params=_cparams(("parallel", "parallel")),
    )(y, y, y, sink.reshape(1, WA_HEADS))


def _pop_max(x, rowid):
    m = jnp.max(x, axis=0, keepdims=True)
    first = jnp.min(jnp.where(x == m, rowid, float(x.shape[0])), axis=0, keepdims=True)
    hit = rowid == first
    return m, first, hit, jnp.where(hit, NEG_INF, x)


def _top_values(x, k):
    half = x.shape[0] // 2
    top, bot = x[:half], x[half:]
    idx = lax.broadcasted_iota(jnp.int32, top.shape, 0).astype(F32)
    swap = bot > top
    hi_v, lo_v = jnp.maximum(top, bot), jnp.minimum(top, bot)
    hi_i, lo_i = jnp.where(swap, idx + half, idx), jnp.where(swap, idx, idx + half)
    for _ in range(k):
        m = jnp.max(hi_v, axis=0, keepdims=True)
        first = jnp.min(jnp.where(hi_v == m, hi_i, float(2 * half)), axis=0, keepdims=True)
        hit = hi_i == first
        hi_v, hi_i = jnp.where(hit, lo_v, hi_v), jnp.where(hit, lo_i, hi_i)
        lo_v = jnp.where(hit, NEG_INF, lo_v)
        yield m, first


_CAND = [(a, b) for a in range(PEER_TOPK) for b in range(PEER_TOPK) if (a + 1) * (b + 1) <= PEER_TOPK]


def _peer_kernel(q_ref, xn_ref, keys_ref, ut_ref, v_ref, h_ref, mod_ref, out_ref,
                 i1_s, i2_s, gt_s, i1t_s, i2t_s, gtt_s, gs_s, w_s, acc_s, *, gate_idx):
    r = pl.program_id(0)
    e = pl.program_id(1)
    tn = h_ref.shape[0]
    nslot = PEER_HEADS * PEER_TOPK

    @pl.when((e == 0) & (r == 0))
    def _init():
        i1_s[...] = jnp.zeros_like(i1_s)
        i2_s[...] = jnp.zeros_like(i2_s)
        gt_s[...] = jnp.zeros_like(gt_s)
        w_s[...] = jnp.zeros_like(w_s)
        acc_s[...] = jnp.zeros_like(acc_s)

    @pl.when(e == 0)
    def _gates():
        i1t_s[...] = i1_s[...].T
        i2t_s[...] = i2_s[...].T
        gtt_s[...] = gt_s[...].T
        sub = lax.broadcasted_iota(jnp.int32, (PEER_NKEYS, nslot), 0).astype(F32)

        def token(n, carry):
            at = jnp.where(sub == i1t_s[pl.ds(n, 1), :], gtt_s[pl.ds(n, 1), :], 0.0).astype(BF)
            bt = jnp.where(sub == i2t_s[pl.ds(n, 1), :], 1.0, 0.0).astype(BF)
            gs_s[pl.ds(pl.multiple_of(n * GS_PITCH, 8), PEER_NKEYS), :] = _nt_dot(at, bt)
            return carry

        lax.fori_loop(0, tn, token, 0, unroll=16)

    def route():
        crow = lax.broadcasted_iota(jnp.int32, (len(_CAND), tn), 0).astype(F32)
        tops, idxs = [], []
        for p in range(2):
            vals, ids = [], []
            for m, first in _top_values(_nt_dot(keys_ref[0, p], q_ref[p]), PEER_TOPK):
                vals.append(m)
                ids.append(first)
                yield
            tops.append(vals)
            idxs.append(ids)
        work = jnp.concatenate([tops[0][a] + tops[1][b] for a, b in _CAND], axis=0)
        ci1 = jnp.concatenate([idxs[0][a] for a, _ in _CAND], axis=0)
        ci2 = jnp.concatenate([idxs[1][b] for _, b in _CAND], axis=0)
        sv, s1, s2 = [], [], []
        for _ in range(PEER_TOPK):
            m, _, hit, work = _pop_max(work, crow)
            sv.append(m)
            s1.append(jnp.max(jnp.where(hit, ci1, -1.0), axis=0, keepdims=True))
            s2.append(jnp.max(jnp.where(hit, ci2, -1.0), axis=0, keepdims=True))
            yield
        ev = jnp.exp(jnp.concatenate(sv, axis=0) - sv[0])
        rows = pl.ds(pl.multiple_of(e * PEER_TOPK, PEER_TOPK), PEER_TOPK)
        i1_s[rows, :] = jnp.concatenate(s1, axis=0)
        i2_s[rows, :] = jnp.concatenate(s2, axis=0)
        gt_s[rows, :] = ev * (0.5 / jnp.sum(ev, axis=0, keepdims=True))

    router = route()
    n_chunk = PEER_TE // PEER_CW
    pops_per_chunk = 3 * PEER_TOPK // n_chunk
    per_chunk = PEER_CW // PEER_NKEYS
    chunks_per_out = n_chunk * PEER_CW // D
    cur = e % 2
    xn = pltpu.bitcast(xn_ref[...], BF)
    for c in range(n_chunk):
        if c % chunks_per_out == 0:
            cols = slice(c // chunks_per_out * PEER_CW, (c // chunks_per_out + 1) * PEER_CW)
            acc_s[:, cols] += _dot(w_s[1 - cur], pltpu.bitcast(v_ref[:, cols], BF))
        a = _dot(xn, pltpu.bitcast(ut_ref[:, c * PEER_CW:(c + 1) * PEER_CW], BF))
        for _ in range(pops_per_chunk):
            next(router, None)
        ws = []
        for k in range(per_chunk):
            i1 = e * (PEER_TE // PEER_NKEYS) + c * per_chunk + k
            g = gs_s[pl.ds(i1, tn, stride=GS_PITCH), :]
            ak = a[:, k * PEER_NKEYS:(k + 1) * PEER_NKEYS]
            ws.append((g * (ak * (1.0 + lax.erf(ak * (0.5 ** 0.5))))).astype(BF))
        w_s[cur, :, c * PEER_CW:(c + 1) * PEER_CW] = jnp.concatenate(ws, axis=1)
    for _ in router:
        pass

    @pl.when(e == 0)
    def _fin():
        gate = mod_ref[0, 0][gate_idx:gate_idx + 1]
        out_ref[...] = h_ref[...] + gate * acc_s[...]
        acc_s[...] = jnp.zeros_like(acc_s)


def _peer_residual(h2, q2, xn2, keys, ut, v, layer, modtab, *, gate_idx, tok_per_batch, ctx_rows):
    ntok = h2.shape[0]
    nexp = ut.shape[2]
    nslot = PEER_HEADS * PEER_TOPK
    tn, te = PEER_TN, PEER_TE
    ntiles = ntok // tn
    tiles_per_batch = tok_per_batch // tn
    assert nexp // te == PEER_HEADS

    nsteps = nexp // te

    def lagged(r, lag):
        return jnp.clip(r - lag, 0, ntiles - 1)

    def mod_map(r, e):
        t = lagged(r, 2)
        return (t // tiles_per_batch, _is_latent_tile(t % tiles_per_batch, ctx_rows // tn), 0, 0)

    return pl.pallas_call(
        functools.partial(_peer_kernel, gate_idx=gate_idx),
        grid=(ntiles + 2, nsteps),
        in_specs=[pl.BlockSpec((2, tn, PEER_DHALF), lambda r, e: (e, lagged(r, 0), 0)),
                  pl.BlockSpec((_words(tn), D), lambda r, e: (lagged(r, 1), 0)),
                  pl.BlockSpec((1, 2, PEER_NKEYS, PEER_DHALF), lambda r, e: (e, 0, 0, 0)),
                  pl.BlockSpec((None, _words(D), te), lambda r, e: (layer, 0, e)),
                  pl.BlockSpec((None, _words(te), D), lambda r, e: (layer, (e + nsteps - 1) % nsteps, 0)),
                  pl.BlockSpec((tn, D), lambda r, e: (lagged(r, 2), 0)),
                  pl.BlockSpec((1, 1, N_MOD, D), mod_map)],
        out_specs=pl.BlockSpec((tn, D), lambda r, e: (lagged(r, 2), 0)),
        out_shape=jax.ShapeDtypeStruct((ntok, D), F32),
        scratch_shapes=[pltpu.VMEM((nslot, tn), F32),
                        pltpu.VMEM((nslot, tn), F32),
                        pltpu.VMEM((nslot, tn), F32),
                        pltpu.VMEM((tn, nslot), F32),
                        pltpu.VMEM((tn, nslot), F32),
                        pltpu.VMEM((tn, nslot), F32),
                        pltpu.VMEM((tn * GS_PITCH, PEER_NKEYS), F32),
                        pltpu.VMEM((2, tn, te), BF),
                        pltpu.VMEM((tn, D), F32)],
        compiler_params=_cparams(("arbitrary", "arbitrary")),
    )(q2, xn2, keys, ut, v, h2, modtab)


def _pack_kernel(w_ref, o_ref, *, transpose):
    w = w_ref[0].T if transpose else w_ref[0]
    o_ref[0] = pltpu.bitcast(w.astype(BF), jnp.uint32)


def _pack_experts(w, *, transpose):
    nl, ne, d = w.shape
    te = 512
    if transpose:
        out_block, out_map, out_dims = (1, _words(d), te), (lambda l, i: (l, 0, i)), (nl, _words(d), ne)
    else:
        out_block, out_map, out_dims = (1, _words(te), d), (lambda l, i: (l, i, 0)), (nl, _words(ne), d)
    return pl.pallas_call(
        functools.partial(_pack_kernel, transpose=transpose),
        grid=(nl, ne // te),
        in_specs=[pl.BlockSpec((1, te, d), lambda l, i: (l, i, 0))],
        out_specs=pl.BlockSpec(out_block, out_map),
        out_shape=jax.ShapeDtypeStruct(out_dims, jnp.uint32),
        compiler_params=_cparams(("parallel", "parallel")),
    )(w)


def _final_kernel(h_ref, g_ref, o_ref):
    x = h_ref[0]
    o_ref[0] = x * lax.rsqrt(jnp.mean(x * x, axis=-1, keepdims=True) + EPS) * g_ref[...]


def _final_norm(h, g):
    bsz, s, _ = h.shape
    return pl.pallas_call(
        _final_kernel,
        grid=(bsz, s // TM),
        in_specs=[pl.BlockSpec((1, TM, D), lambda b, i: (b, i, 0)),
                  pl.BlockSpec((1, D), lambda b, i: (0, 0))],
        out_specs=pl.BlockSpec((1, TM, D), lambda b, i: (b, i, 0)),
        out_shape=jax.ShapeDtypeStruct((bsz, s, D), F32),
        compiler_params=_cparams(("parallel", "parallel")),
    )(h, g.reshape(1, D))


def _rot_half_cols(w, dh):
    d, n = w.shape
    wh = w.reshape(d, n // dh, 2, dh // 2)
    return jnp.concatenate([-wh[:, :, 1], wh[:, :, 0]], axis=2).reshape(d, n)


def _rope_tables(s, dh, width):
    rows = s // GRID_W
    r = jnp.repeat(jnp.arange(rows, dtype=F32), GRID_W)
    col = jnp.broadcast_to(jnp.arange(GRID_W, dtype=F32), (rows, GRID_W)).reshape(-1)
    nf = dh // 4
    inv = ROPE_THETA ** (-jnp.arange(nf, dtype=F32) / nf)
    ang = jnp.concatenate([r[:, None] * inv, col[:, None] * inv], axis=-1)
    cos = jnp.concatenate([jnp.ones((CTX, dh // 2), F32), jnp.cos(ang)], axis=0)
    sin = jnp.concatenate([jnp.zeros((CTX, dh // 2), F32), jnp.sin(ang)], axis=0)
    reps = width // (dh // 2)
    return jnp.tile(cos, (1, reps)), jnp.tile(sin, (1, reps))


def _interleave_heads(a, nh, dh):
    lead = a.shape[:-1]
    return a.reshape(lead + (2, nh, dh)).swapaxes(-3, -2).reshape(lead + (2 * nh * dh,))


def kernel(x, c, ctx, c_ctx, mod_w, mod_b, norm1_g, norm2_g, ev_w_in, ev_ml_conv_w, ev_ml_conv_b,
           ev_ml_gate_b, ev_ml_norm_g, ev_da_lam, ev_da_norm_g, ev_w_out, od_w_in, od_sink, od_w_out,
           pr_w_q, pr_keys, pr_u, pr_v, final_g):
    bsz, s, _ = x.shape
    depth = mod_w.shape[0]
    t = CTX + s
    ml_w = ML_HEADS * ML_DV
    da_w = DA_HEADS * DA_DV
    qk_w = 2 * ML_HEADS * ML_DQK
    da_qw = DA_HEADS * 2 * DA_DQK

    h = jnp.concatenate([ctx, x], axis=1)

    rows = -(-(bsz + 1) // 8) * 8
    cc = jnp.zeros((rows, D), F32).at[:bsz].set(c).at[bsz].set(c_ctx)
    mods = _modulation(cc, mod_w, mod_b)
    mod_lat = mods[:, :bsz].reshape(depth, bsz, N_MOD, D)
    mod_ctx = jnp.broadcast_to(mods[:, bsz].reshape(depth, 1, N_MOD, D), (depth, bsz, N_MOD, D))
    modtabs = jnp.stack([mod_ctx, mod_lat], axis=2)

    wa_rope_w = (WA_HEADS + WA_KV_HEADS) * WA_DH
    cos_t, sin_t = _rope_tables(s, WA_DH, wa_rope_w)
    ut_words = _pack_experts(pr_u, transpose=True)
    v_words = _pack_experts(pr_v, transpose=False)

    for i in range(depth):
        j = i // 2
        modtab = modtabs[i]
        if i % 2 == 0:
            lam_init = 0.8 - 0.6 * math.exp(-0.3 * i)
            w = ev_w_in[j]
            o0 = 0
            w_mqk = w[:, o0:o0 + qk_w]; o0 += qk_w
            w_mv = w[:, o0:o0 + ml_w]; o0 += ml_w
            w_mo = w[:, o0:o0 + ml_w]; o0 += ml_w
            w_g = w[:, o0:o0 + 4 * ML_HEADS]; o0 += 4 * ML_HEADS
            w_dq = w[:, o0:o0 + da_qw]; o0 += da_qw
            w_dk = w[:, o0:o0 + da_qw]; o0 += da_qw
            w_dv = w[:, o0:o0 + da_w]
            w_rope = jnp.concatenate([w_dq, w_dk], axis=1)
            gpad = jnp.zeros((D, 128 - 4 * ML_HEADS), F32)
            wm = jnp.concatenate([w_rope, _interleave_heads(w_mqk, ML_HEADS, ML_DQK), w_mv, w_mo, w_dv, w_g, gpad],
                                 axis=1).astype(BF)
            wr = _rot_half_cols(w_rope, DA_DQK).astype(BF)
            nr = 2 * da_qw
            col_mqk = nr
            col_mv = col_mqk + qk_w
            col_mo = col_mv + ml_w
            col_dv = col_mo + ml_w
            col_g = col_dv + da_w
            (y,) = _norm_mod_proj(h, modtab, norm1_g[i], wm, wr, cos_t[:, :nr], sin_t[:, :nr],
                                  shift_idx=0, scale_idx=1)
            gates_t = jnp.swapaxes(y[:, :, col_g:col_g + 4 * ML_HEADS], 1, 2)
            gk, qs = _gate_scan(gates_t, ev_ml_gate_b[j])
            qs = qs.reshape(bsz, 4, ML_HEADS, t).transpose(0, 2, 3, 1)
            ml = _mlstm(y, gk, qs, _interleave_heads(ev_ml_conv_w[j], ML_HEADS, ML_DQK),
                        _interleave_heads(ev_ml_conv_b[j], ML_HEADS, ML_DQK), ev_ml_norm_g[j],
                        col_qk=col_mqk, col_v=col_mv, col_o=col_mo)
            da = _diff_attention(y, ev_da_lam[j], ev_da_norm_g[j], col_q=0, col_k=da_qw, col_v=col_dv,
                                 lam_init=lam_init)
            w_out = ev_w_out[j].astype(BF)
            h = _outproj_residual(h, modtab, [(ml, w_out[:ml_w]), (da, w_out[ml_w:])], gate_idx=2)
        else:
            w = od_w_in[j]
            wm = w.astype(BF)
            wr = _rot_half_cols(w[:, :wa_rope_w], WA_DH).astype(BF)
            (y,) = _norm_mod_proj(h, modtab, norm1_g[i], wm, wr, cos_t, sin_t, shift_idx=0, scale_idx=1)
            att = _window_attention(y, od_sink[j], col_q=0, col_k=WA_HEADS * WA_DH,
                                    col_v=WA_HEADS * WA_DH + WA_KV_HEADS * WA_DH)
            h = _outproj_residual(h, modtab, [(att, od_w_out[j].astype(BF))], gate_idx=2)
        ctx_rows = CTX if i < depth - 1 else 0
        hp = h if ctx_rows else h[:, CTX:]
        tp = hp.shape[1]
        q, xn = _norm_mod_proj(hp, modtab, norm2_g[i], pr_w_q[i].astype(BF), shift_idx=3, scale_idx=4,
                               want_xn=True, split=PEER_DHALF, ctx_rows=ctx_rows)
        h = _peer_residual(hp.reshape(bsz * tp, D), q, xn.reshape(-1, D), pr_keys[i].astype(BF),
                           ut_words, v_words, i, modtab,
                           gate_idx=5, tok_per_batch=tp, ctx_rows=ctx_rows).reshape(bsz, tp, D)
    return _final_norm(h, final_g)
```

```python
import functools
import math

import jax
import jax.numpy as jnp
from jax import lax
from jax.experimental import pallas as pl
from jax.experimental.pallas import tpu as pltpu

D = 1024
CTX = 256
GRID_W = 64
EPS = 1e-6
ROPE_THETA = 10000.0
N_MOD = 6

ML_HEADS = 4
ML_DQK = 64
ML_DV = 128
DA_HEADS = 4
DA_DQK = 64
DA_DV = 128
WA_HEADS = 16
WA_KV_HEADS = 4
WA_GROUP = 4
WA_DH = 64
WINDOW = 128

PEER_HEADS = 8
PEER_NKEYS = 128
PEER_DHALF = 128
PEER_TOPK = 16

TM = 256
TQ_ML = 256
TQ_DA = 256
TQ_WA = 128
PEER_TN = 256
PEER_TE = 2048
PEER_CW = 256
GS_PITCH = 72
VMEM_LIMIT = 56 * 1024 * 1024

BF = jnp.bfloat16
F32 = jnp.float32
NEG_INF = float("-inf")


def _cparams(sem):
    return pltpu.CompilerParams(dimension_semantics=sem, vmem_limit_bytes=VMEM_LIMIT)


def _nt_dot(a, b):
    return lax.dot_general(a, b, (((1,), (1,)), ((), ())), preferred_element_type=F32)


def _tn_dot(a, b):
    return lax.dot_general(a, b, (((0,), (0,)), ((), ())), preferred_element_type=F32)


def _dot(a, b):
    return jnp.dot(a, b, preferred_element_type=F32)


def _words(rows):
    return rows * jnp.dtype(BF).itemsize // 4


def _mod_kernel(cc_ref, w_ref, b_ref, o_ref):
    a = jax.nn.silu(cc_ref[...]).astype(BF)
    o_ref[0] = _dot(a, w_ref[0].astype(BF)) + b_ref[0]


def _modulation(cc, mod_w, mod_b):
    depth = mod_w.shape[0]
    rows = cc.shape[0]
    tn = 1536
    return pl.pallas_call(
        _mod_kernel,
        grid=(depth, N_MOD * D // tn),
        in_specs=[pl.BlockSpec((rows, D), lambda l, n: (0, 0)),
                  pl.BlockSpec((1, D, tn), lambda l, n: (l, 0, n)),
                  pl.BlockSpec((1, 1, tn), lambda l, n: (l, 0, n))],
        out_specs=pl.BlockSpec((1, rows, tn), lambda l, n: (l, 0, n)),
        out_shape=jax.ShapeDtypeStruct((depth, rows, N_MOD * D), F32),
        compiler_params=_cparams(("parallel", "parallel")),
    )(cc, mod_w, mod_b.reshape(depth, 1, N_MOD * D))


def _is_latent_tile(tile, ctx_tiles):
    return jnp.minimum(tile // ctx_tiles, 1) if ctx_tiles else 1


def _mod_spec(ctx_rows=CTX):
    return pl.BlockSpec((1, 1, N_MOD, D), lambda b, t: (b, _is_latent_tile(t, ctx_rows // TM), 0, 0))


def _proj_kernel(*refs, shift_idx, scale_idx, nrope, want_xn, split):
    it = iter(refs)
    h_ref, mod_ref, g_ref, wm_ref = next(it), next(it), next(it), next(it)
    if nrope:
        wr_ref, cos_ref, sin_ref = next(it), next(it), next(it)
    y_ref = next(it)
    x = h_ref[0]
    xn = x * lax.rsqrt(jnp.mean(x * x, axis=-1, keepdims=True) + EPS) * g_ref[...]
    mod = mod_ref[0, 0]
    xm = xn * (1.0 + mod[scale_idx:scale_idx + 1]) + mod[shift_idx:shift_idx + 1]
    xb = xm.astype(BF)
    acc = _dot(xb, wm_ref[...])
    if nrope:
        rot = _dot(xb, wr_ref[...])
        y_ref[0, :, :nrope] = acc[:, :nrope] * cos_ref[...] + rot * sin_ref[...]
        y_ref[0, :, nrope:] = acc[:, nrope:]
    elif split:
        for c in range(acc.shape[1] // split):
            y_ref[c] = acc[:, c * split:(c + 1) * split].astype(y_ref.dtype)
    else:
        y_ref[0] = acc
    if want_xn:
        next(it)[0] = pltpu.bitcast(xb, jnp.uint32)


def _norm_mod_proj(h, modtab, g, wm, wr=None, cos=None, sin=None, *, shift_idx, scale_idx, want_xn=False,
                   split=0, ctx_rows=CTX):
    bsz, t, _ = h.shape
    nm = wm.shape[1]
    nrope = 0 if wr is None else wr.shape[1]
    in_specs = [pl.BlockSpec((1, TM, D), lambda b, i: (b, i, 0)),
                _mod_spec(ctx_rows),
                pl.BlockSpec((1, D), lambda b, i: (0, 0)),
                pl.BlockSpec((D, nm), lambda b, i: (0, 0))]
    args = [h, modtab, g.reshape(1, D), wm]
    if nrope:
        in_specs += [pl.BlockSpec((D, nrope), lambda b, i: (0, 0)),
                     pl.BlockSpec((TM, nrope), lambda b, i: (i, 0)),
                     pl.BlockSpec((TM, nrope), lambda b, i: (i, 0))]
        args += [wr, cos, sin]
    out_specs = [pl.BlockSpec((1, TM, nm), lambda b, i: (b, i, 0))]
    out_shape = [jax.ShapeDtypeStruct((bsz, t, nm), F32)]
    if split:
        out_specs = [pl.BlockSpec((nm // split, TM, split), lambda b, i: (0, b * (t // TM) + i, 0))]
        out_shape = [jax.ShapeDtypeStruct((nm // split, bsz * t, split), BF)]
    if want_xn:
        out_specs.append(pl.BlockSpec((1, _words(TM), D), lambda b, i: (b, i, 0)))
        out_shape.append(jax.ShapeDtypeStruct((bsz, _words(t), D), jnp.uint32))
    return pl.pallas_call(
        functools.partial(_proj_kernel, shift_idx=shift_idx, scale_idx=scale_idx, nrope=nrope, want_xn=want_xn,
                          split=split),
        grid=(bsz, t // TM), in_specs=in_specs, out_specs=out_specs, out_shape=out_shape,
        compiler_params=_cparams(("parallel", "parallel")),
    )(*args)


def _outproj_kernel(*refs, gate_idx, n_src):
    h_ref, mod_ref = refs[0], refs[1]
    o_ref = refs[2 + 2 * n_src]
    acc = None
    for s in range(n_src):
        part = _dot(refs[2 + 2 * s][0].astype(BF), refs[3 + 2 * s][...])
        acc = part if acc is None else acc + part
    gate = mod_ref[0, 0][gate_idx:gate_idx + 1]
    o_ref[0] = h_ref[0] + gate * acc


def _outproj_residual(h, modtab, srcs, *, gate_idx):
    bsz, t, _ = h.shape
    in_specs = [pl.BlockSpec((1, TM, D), lambda b, i: (b, i, 0)), _mod_spec()]
    args = [h, modtab]
    for x, w in srcs:
        k = x.shape[-1]
        in_specs += [pl.BlockSpec((1, TM, k), lambda b, i: (b, i, 0)),
                     pl.BlockSpec((k, D), lambda b, i: (0, 0))]
        args += [x, w]
    return pl.pallas_call(
        functools.partial(_outproj_kernel, gate_idx=gate_idx, n_src=len(srcs)),
        grid=(bsz, t // TM), in_specs=in_specs,
        out_specs=pl.BlockSpec((1, TM, D), lambda b, i: (b, i, 0)),
        out_shape=jax.ShapeDtypeStruct((bsz, t, D), F32),
        compiler_params=_cparams(("parallel", "parallel")),
    )(*args)


def _lane_scan(x, op, ident, lane, reverse):
    for k in range(7):
        s = 1 << k
        if reverse:
            sh = pltpu.roll(x, 128 - s, 1)
            x = op(x, jnp.where(lane < 128 - s, sh, ident))
        else:
            sh = pltpu.roll(x, s, 1)
            x = op(x, jnp.where(lane >= s, sh, ident))
    return x


def _gate_scan_kernel(g_ref, b_ref, gk_ref, qs_ref, *, t):
    nb = t // 128
    nh = ML_HEADS
    lane = lax.broadcasted_iota(jnp.int32, (nh, 128), 1)
    pre = g_ref[0] + b_ref[...]
    for d in range(2):
        ig = pre[(2 * d) * nh:(2 * d + 1) * nh]
        lf = jax.nn.log_sigmoid(pre[(2 * d + 1) * nh:(2 * d + 2) * nh])
        if d == 0:
            order = list(range(nb))
        else:
            order = list(range(CTX // 128 - 1, -1, -1)) + list(range(nb - 1, CTX // 128 - 1, -1))
        carry_b = jnp.zeros((nh, 1), F32)
        carry_m = jnp.full((nh, 1), NEG_INF, F32)
        edge = 0 if d else 127
        for blk in order:
            sl = slice(blk * 128, (blk + 1) * 128)
            bb = _lane_scan(lf[:, sl], jnp.add, 0.0, lane, bool(d)) + carry_b
            gg = ig[:, sl] - bb
            mm = jnp.maximum(_lane_scan(gg, jnp.maximum, NEG_INF, lane, bool(d)), carry_m)
            carry_b = bb[:, edge:edge + 1]
            carry_m = mm[:, edge:edge + 1]
            gk_ref[0, d * nh:(d + 1) * nh, sl] = gg
            qs_ref[0, (2 * d) * nh:(2 * d + 1) * nh, sl] = mm
            qs_ref[0, (2 * d + 1) * nh:(2 * d + 2) * nh, sl] = bb + mm


def _gate_scan(gates_t, gate_b):
    bsz, ng, t = gates_t.shape
    return pl.pallas_call(
        functools.partial(_gate_scan_kernel, t=t),
        grid=(bsz,),
        in_specs=[pl.BlockSpec((1, ng, t), lambda b: (b, 0, 0)),
                  pl.BlockSpec((ng, 1), lambda b: (0, 0))],
        out_specs=[pl.BlockSpec((1, 2 * ML_HEADS, t), lambda b: (b, 0, 0)),
                   pl.BlockSpec((1, 4 * ML_HEADS, t), lambda b: (b, 0, 0))],
        out_shape=[jax.ShapeDtypeStruct((bsz, 2 * ML_HEADS, t), F32),
                   jax.ShapeDtypeStruct((bsz, 4 * ML_HEADS, t), F32)],
        compiler_params=_cparams(("parallel",)),
    )(gates_t, gate_b.reshape(ng, 1))


def _mlstm_kernel(qk_ref, v_ref, o_ref, gk_ref, qs_ref, cw_ref, cb_ref, ng_ref, out_ref,
                  q_s, k_s, v_s, *, t):
    hd = pl.program_id(1)
    x = qk_ref[0]
    row = lax.broadcasted_iota(jnp.int32, (t, 1), 0)
    xp = jnp.where((row == 0) | (row == CTX), 0.0, pltpu.roll(x, 1, 0))
    xn = jnp.where((row == CTX - 1) | (row == t - 1), 0.0, pltpu.roll(x, t - 1, 0))
    cw = cw_ref[...]
    y = cb_ref[...] + xp * cw[0:1] + x * cw[1:2] + xn * cw[2:3]
    y = jax.nn.silu(y)
    q_s[...] = y[:, :ML_DQK].astype(BF)
    k_s[...] = (y[:, ML_DQK:] * (ML_DQK ** -0.5)).astype(BF)
    v_s[...] = v_ref[0].astype(BF)
    lower = lax.broadcasted_iota(jnp.int32, (1, TQ_ML), 1) <= lax.broadcasted_iota(jnp.int32, (TQ_ML, 1), 0)
    upper = lax.broadcasted_iota(jnp.int32, (1, TQ_ML), 1) >= lax.broadcasted_iota(jnp.int32, (TQ_ML, 1), 0)

    for qi in range(t // TQ_ML):
        r0 = qi * TQ_ML
        rows = slice(r0, r0 + TQ_ML)
        s = _nt_dot(q_s[rows, :], k_s[...])
        qs = qs_ref[0, 0, rows, :]
        fwd = ([(0, r0, None)] if r0 else []) + [(r0, r0 + TQ_ML, lower)]
        seg_end = CTX if r0 < CTX else t
        bwd = ([(0, CTX, None)] if r0 >= CTX else []) + [(r0, r0 + TQ_ML, upper)]
        bwd += [(r0 + TQ_ML, seg_end, None)] if r0 + TQ_ML < seg_end else []
        hsum = None
        for d, pieces in enumerate((fwd, bwd)):
            gk = gk_ref[0, pl.ds(d * ML_HEADS + hd, 1), :]
            den = num = None
            for a, b, tri in pieces:
                z = gk[:, a:b] - qs[:, 2 * d:2 * d + 1]
                p = s[:, a:b] * jnp.exp(z if tri is None else jnp.where(tri, z, NEG_INF))
                dp = jnp.sum(p, axis=-1, keepdims=True)
                np_ = _dot(p.astype(BF), v_s[a:b, :])
                den = dp if den is None else den + dp
                num = np_ if num is None else num + np_
            hdir = num / jnp.maximum(jnp.abs(den), jnp.exp(-qs[:, 2 * d + 1:2 * d + 2]))
            hsum = hdir if hsum is None else hsum + hdir
        yn = hsum * lax.rsqrt(jnp.mean(hsum * hsum, axis=-1, keepdims=True) + EPS) * ng_ref[...]
        out_ref[0, rows, :] = yn * jax.nn.sigmoid(o_ref[0, rows, :])


def _mlstm(y, gk, qs, conv_w, conv_b, norm_g, *, col_qk, col_v, col_o):
    bsz, t, _ = y.shape
    w = ML_HEADS * ML_DV
    return pl.pallas_call(
        functools.partial(_mlstm_kernel, t=t),
        grid=(bsz, ML_HEADS),
        in_specs=[pl.BlockSpec((1, t, 128), lambda b, h: (b, 0, col_qk // 128 + h)),
                  pl.BlockSpec((1, t, 128), lambda b, h: (b, 0, col_v // 128 + h)),
                  pl.BlockSpec((1, t, 128), lambda b, h: (b, 0, col_o // 128 + h)),
                  pl.BlockSpec((1, 2 * ML_HEADS, t), lambda b, h: (b, 0, 0)),
                  pl.BlockSpec((1, 1, t, 4), lambda b, h: (b, h, 0, 0)),
                  pl.BlockSpec((3, 128), lambda b, h: (0, h)),
                  pl.BlockSpec((1, 128), lambda b, h: (0, h)),
                  pl.BlockSpec((1, 128), lambda b, h: (0, h))],
        out_specs=pl.BlockSpec((1, t, 128), lambda b, h: (b, 0, h)),
        out_shape=jax.ShapeDtypeStruct((bsz, t, w), F32),
        scratch_shapes=[pltpu.VMEM((t, ML_DQK), BF), pltpu.VMEM((t, ML_DQK), BF), pltpu.VMEM((t, ML_DV), BF)],
        compiler_params=_cparams(("parallel", "parallel")),
    )(y, y, y, gk, qs, conv_w, conv_b.reshape(1, w), norm_g.reshape(1, w))


def _da_kernel(q_ref, k_ref, v_ref, lam_ref, ng_ref, out_ref, *, t, lam_init):
    qt = pl.program_id(2)
    lp = lam_ref[...]
    lam = (jnp.exp(jnp.sum(lp[0:1] * lp[1:2], axis=-1, keepdims=True))
           - jnp.exp(jnp.sum(lp[2:3] * lp[3:4], axis=-1, keepdims=True)) + lam_init)
    q = q_ref[0] * (DA_DQK ** -0.5)

    def attend(nk):
        v = v_ref[0, :nk, :].astype(BF)
        outs = []
        for m in range(2):
            qm = q[:, m * DA_DQK:(m + 1) * DA_DQK].astype(BF)
            km = k_ref[0, :nk, m * DA_DQK:(m + 1) * DA_DQK].astype(BF)
            s = _nt_dot(qm, km)
            e = jnp.exp(s - jnp.max(s, axis=-1, keepdims=True))
            outs.append(_dot(e.astype(BF), v) / jnp.sum(e, axis=-1, keepdims=True))
        o = outs[0] - lam * outs[1]
        yn = o * lax.rsqrt(jnp.mean(o * o, axis=-1, keepdims=True) + EPS) * ng_ref[...]
        out_ref[0] = yn * (1.0 - lam_init)

    @pl.when(qt < CTX // TQ_DA)
    def _():
        attend(CTX)

    @pl.when(qt >= CTX // TQ_DA)
    def _():
        attend(t)


def _diff_attention(y, da_lam, norm_g, *, col_q, col_k, col_v, lam_init):
    bsz, t, _ = y.shape
    w = DA_HEADS * DA_DV
    return pl.pallas_call(
        functools.partial(_da_kernel, t=t, lam_init=lam_init),
        grid=(bsz, DA_HEADS, t // TQ_DA),
        in_specs=[pl.BlockSpec((1, TQ_DA, 128), lambda b, h, i: (b, i, col_q // 128 + h)),
                  pl.BlockSpec((1, t, 128), lambda b, h, i: (b, 0, col_k // 128 + h)),
                  pl.BlockSpec((1, t, 128), lambda b, h, i: (b, 0, col_v // 128 + h)),
                  pl.BlockSpec((4, DA_DQK), lambda b, h, i: (0, 0)),
                  pl.BlockSpec((1, 128), lambda b, h, i: (0, h))],
        out_specs=pl.BlockSpec((1, TQ_DA, 128), lambda b, h, i: (b, i, h)),
        out_shape=jax.ShapeDtypeStruct((bsz, t, w), F32),
        compiler_params=_cparams(("parallel", "parallel", "parallel")),
    )(y, y, y, da_lam, norm_g.reshape(1, w))


def _wa_kernel(q_ref, k_ref, v_ref, sink_ref, out_ref, *, t):
    i = pl.program_id(1)
    n_ctx_blk = CTX // TQ_WA
    n_lat = t - CTX
    span = 3 * WINDOW

    def attend(local):
        if local:
            lat0 = jnp.clip((i - n_ctx_blk - 1) * TQ_WA, 0, n_lat - span)
            start = pl.multiple_of(CTX + lat0, TQ_WA)
            qcol = lax.broadcasted_iota(jnp.int32, (1, WA_GROUP, TQ_WA), 2).reshape(1, WA_GROUP * TQ_WA)
            kpos = lat0 + lax.broadcasted_iota(jnp.int32, (span, 1), 0)
            mask = jnp.abs((i - n_ctx_blk) * TQ_WA + qcol - kpos) <= WINDOW
        for g in range(WA_KV_HEADS):
            cs = slice(g * WA_DH, (g + 1) * WA_DH)
            heads = range(g * WA_GROUP, (g + 1) * WA_GROUP)
            kc = k_ref[0, :CTX, cs].astype(BF)
            vc = v_ref[0, :CTX, cs].astype(BF)
            q = jnp.concatenate([q_ref[0, :, hd * WA_DH:(hd + 1) * WA_DH] for hd in heads], axis=0)
            q = (q * (WA_DH ** -0.5)).astype(BF)
            sink = jnp.concatenate([jnp.broadcast_to(sink_ref[:, hd:hd + 1], (1, TQ_WA)) for hd in heads], axis=1)
            sc = _nt_dot(kc, q)
            mx = jnp.maximum(jnp.max(sc, axis=0, keepdims=True), sink)
            if local:
                kl = k_ref[0, pl.ds(start, span), cs].astype(BF)
                vl = v_ref[0, pl.ds(start, span), cs].astype(BF)
                sl = jnp.where(mask, _nt_dot(kl, q), NEG_INF)
                mx = jnp.maximum(mx, jnp.max(sl, axis=0, keepdims=True))
            ec = jnp.exp(sc - mx)
            den = jnp.sum(ec, axis=0, keepdims=True) + jnp.exp(sink - mx)
            num = _tn_dot(vc, ec.astype(BF))
            if local:
                el = jnp.exp(sl - mx)
                den = den + jnp.sum(el, axis=0, keepdims=True)
                num = num + _tn_dot(vl, el.astype(BF))
            o = (num / den).T
            out_ref[0, :, g * WA_GROUP * WA_DH:(g + 1) * WA_GROUP * WA_DH] = jnp.concatenate(
                [o[hh * TQ_WA:(hh + 1) * TQ_WA] for hh in range(WA_GROUP)], axis=1)

    @pl.when(i < n_ctx_blk)
    def _():
        attend(False)

    @pl.when(i >= n_ctx_blk)
    def _():
        attend(True)


def _window_attention(y, sink, *, col_q, col_k, col_v):
    bsz, t, _ = y.shape
    qw = WA_HEADS * WA_DH
    kw = WA_KV_HEADS * WA_DH
    return pl.pallas_call(
        functools.partial(_wa_kernel, t=t),
        grid=(bsz, t // TQ_WA),
        in_specs=[pl.BlockSpec((1, TQ_WA, qw), lambda b, i: (b, i, col_q // qw)),
                  pl.BlockSpec((1, t, kw), lambda b, i: (b, 0, col_k // kw)),
                  pl.BlockSpec((1, t, kw), lambda b, i: (b, 0, col_v // kw)),
                  pl.BlockSpec((1, WA_HEADS), lambda b, i: (0, 0))],
        out_specs=pl.BlockSpec((1, TQ_WA, qw), lambda b, i: (b, i, 0)),
        out_shape=jax.ShapeDtypeStruct((bsz, t, qw), F32),
        compiler_params=_cparams(("parallel", "parallel")),
    )(y, y, y, sink.reshape(1, WA_HEADS))


def _pop_max(x, rowid):
    m = jnp.max(x, axis=0, keepdims=True)
    first = jnp.min(jnp.where(x == m, rowid, float(x.shape[0])), axis=0, keepdims=True)
    hit = rowid == first
    return m, first, hit, jnp.where(hit, NEG_INF, x)


def _top_values(x, k):
    half = x.shape[0] // 2
    top, bot = x[:half], x[half:]
    idx = lax.broadcasted_iota(jnp.int32, top.shape, 0).astype(F32)
    swap = bot > top
    hi_v, lo_v = jnp.maximum(top, bot), jnp.minimum(top, bot)
    hi_i, lo_i = jnp.where(swap, idx + half, idx), jnp.where(swap, idx, idx + half)
    for _ in range(k):
        m = jnp.max(hi_v, axis=0, keepdims=True)
        first = jnp.min(jnp.where(hi_v == m, hi_i, float(2 * half)), axis=0, keepdims=True)
        hit = hi_i == first
        hi_v, hi_i = jnp.where(hit, lo_v, hi_v), jnp.where(hit, lo_i, hi_i)
        lo_v = jnp.where(hit, NEG_INF, lo_v)
        yield m, first


_CAND = [(a, b) for a in range(PEER_TOPK) for b in range(PEER_TOPK) if (a + 1) * (b + 1) <= PEER_TOPK]


def _peer_kernel(q_ref, xn_ref, keys_ref, ut_ref, v_ref, h_ref, mod_ref, out_ref,
                 i1_s, i2_s, gt_s, i1t_s, i2t_s, gtt_s, gsa_s, gsb_s, w_s, acc_s, *, gate_idx):
    r = pl.program_id(0)
    e = pl.program_id(1)
    tn = h_ref.shape[0]
    nslot = PEER_HEADS * PEER_TOPK

    @pl.when((e == 0) & (r == 0))
    def _init():
        i1_s[...] = jnp.zeros_like(i1_s)
        i2_s[...] = jnp.zeros_like(i2_s)
        gt_s[...] = jnp.zeros_like(gt_s)
        gsa_s[...] = jnp.zeros_like(gsa_s)
        gsb_s[...] = jnp.zeros_like(gsb_s)
        w_s[...] = jnp.zeros_like(w_s)
        acc_s[...] = jnp.zeros_like(acc_s)

    @pl.when(e == 0)
    def _slots():
        i1t_s[...] = i1_s[...].T
        i2t_s[...] = i2_s[...].T
        gtt_s[...] = gt_s[...].T

    def route(tok):
        nt = tok.stop - tok.start
        crow = lax.broadcasted_iota(jnp.int32, (len(_CAND), nt), 0).astype(F32)
        tops, idxs = [], []
        for p in range(2):
            vals, ids = [], []
            for m, first in _top_values(_nt_dot(keys_ref[0, p], q_ref[p, tok, :]), PEER_TOPK):
                vals.append(m)
                ids.append(first)
                yield
            tops.append(vals)
            idxs.append(ids)
        work = jnp.concatenate([tops[0][a] + tops[1][b] for a, b in _CAND], axis=0)
        ci1 = jnp.concatenate([idxs[0][a] for a, _ in _CAND], axis=0)
        ci2 = jnp.concatenate([idxs[1][b] for _, b in _CAND], axis=0)
        sv, s1, s2 = [], [], []
        for _ in range(PEER_TOPK):
            m, _, hit, work = _pop_max(work, crow)
            sv.append(m)
            s1.append(jnp.max(jnp.where(hit, ci1, -1.0), axis=0, keepdims=True))
            s2.append(jnp.max(jnp.where(hit, ci2, -1.0), axis=0, keepdims=True))
            yield
        ev = jnp.exp(jnp.concatenate(sv, axis=0) - sv[0])
        rows = pl.ds(pl.multiple_of(e * PEER_TOPK, PEER_TOPK), PEER_TOPK)
        i1_s[rows, tok] = jnp.concatenate(s1, axis=0)
        i2_s[rows, tok] = jnp.concatenate(s2, axis=0)
        gt_s[rows, tok] = ev * (0.5 / jnp.sum(ev, axis=0, keepdims=True))

    def build_gates(gs_write):
        sub = lax.broadcasted_iota(jnp.int32, (PEER_NKEYS, nslot), 0).astype(F32)
        per_step = tn // PEER_HEADS
        for j in range(per_step):
            n = e * per_step + j
            at = jnp.where(sub == i1t_s[pl.ds(n, 1), :], gtt_s[pl.ds(n, 1), :], 0.0).astype(BF)
            bt = jnp.where(sub == i2t_s[pl.ds(n, 1), :], 1.0, 0.0).astype(BF)
            g = pltpu.bitcast(_nt_dot(at, bt).astype(BF), jnp.uint32)
            gs_write[pl.ds(pl.multiple_of(n * GS_PITCH, 8), _words(PEER_NKEYS)), :] = g
            yield

    def step(gs_read, gs_write):
        routers = [route(slice(k * 128, (k + 1) * 128)) for k in range(tn // 128)]
        gater = build_gates(gs_write)
        n_chunk = PEER_TE // PEER_CW
        pops_per_chunk = 3 * PEER_TOPK // n_chunk
        gates_per_chunk = tn // PEER_HEADS // n_chunk
        per_chunk = PEER_CW // PEER_NKEYS
        chunks_per_out = n_chunk * PEER_CW // D
        cur = e % 2
        xn = pltpu.bitcast(xn_ref[...], BF)
        for router in routers:
            next(router)
        for c in range(n_chunk):
            if c % chunks_per_out == 0:
                cols = slice(c // chunks_per_out * PEER_CW, (c // chunks_per_out + 1) * PEER_CW)
                acc_s[:, cols] += _dot(w_s[1 - cur], pltpu.bitcast(v_ref[:, cols], BF))
            a = _dot(xn, pltpu.bitcast(ut_ref[:, c * PEER_CW:(c + 1) * PEER_CW], BF))
            for _ in range(pops_per_chunk):
                for router in routers:
                    next(router, None)
            for _ in range(gates_per_chunk):
                next(gater, None)
            ws = []
            for k in range(0, per_chunk, 2):
                pair = (e * (PEER_TE // PEER_NKEYS) + c * per_chunk + k) // 2
                words = gs_read[pl.ds(pair, tn, stride=GS_PITCH), :]
                for half, g in enumerate((pltpu.bitcast(words << 16, F32),
                                          pltpu.bitcast(words & jnp.uint32(0xFFFF0000), F32))):
                    ak = a[:, (k + half) * PEER_NKEYS:(k + half + 1) * PEER_NKEYS]
                    ws.append((g * (ak * (1.0 + lax.erf(ak * (0.5 ** 0.5))))).astype(BF))
            w_s[cur, :, c * PEER_CW:(c + 1) * PEER_CW] = jnp.concatenate(ws, axis=1)
        for gen in routers + [gater]:
            for _ in gen:
                pass

    @pl.when(r % 2 == 0)
    def _even():
        step(gsa_s, gsb_s)

    @pl.when(r % 2 == 1)
    def _odd():
        step(gsb_s, gsa_s)

    @pl.when(e == 0)
    def _fin():
        gate = mod_ref[0, 0][gate_idx:gate_idx + 1]
        out_ref[...] = h_ref[...] + gate * acc_s[...]
        acc_s[...] = jnp.zeros_like(acc_s)


def _peer_residual(h2, q2, xn2, keys, ut, v, layer, modtab, *, gate_idx, tok_per_batch, ctx_rows):
    ntok = h2.shape[0]
    nexp = ut.shape[2]
    nslot = PEER_HEADS * PEER_TOPK
    tn, te = PEER_TN, PEER_TE
    ntiles = ntok // tn
    tiles_per_batch = tok_per_batch // tn
    assert nexp // te == PEER_HEADS

    nsteps = nexp // te

    def lagged(r, lag):
        return jnp.clip(r - lag, 0, ntiles - 1)

    def mod_map(r, e):
        t = lagged(r, 3)
        return (t // tiles_per_batch, _is_latent_tile(t % tiles_per_batch, ctx_rows // tn), 0, 0)

    return pl.pallas_call(
        functools.partial(_peer_kernel, gate_idx=gate_idx),
        grid=(ntiles + 3, nsteps),
        in_specs=[pl.BlockSpec((2, tn, PEER_DHALF), lambda r, e: (e, lagged(r, 0), 0)),
                  pl.BlockSpec((_words(tn), D), lambda r, e: (lagged(r, 2), 0)),
                  pl.BlockSpec((1, 2, PEER_NKEYS, PEER_DHALF), lambda r, e: (e, 0, 0, 0)),
                  pl.BlockSpec((None, _words(D), te), lambda r, e: (layer, 0, e)),
                  pl.BlockSpec((None, _words(te), D), lambda r, e: (layer, (e + nsteps - 1) % nsteps, 0)),
                  pl.BlockSpec((tn, D), lambda r, e: (lagged(r, 3), 0)),
                  pl.BlockSpec((1, 1, N_MOD, D), mod_map)],
        out_specs=pl.BlockSpec((tn, D), lambda r, e: (lagged(r, 3), 0)),
        out_shape=jax.ShapeDtypeStruct((ntok, D), F32),
        scratch_shapes=[pltpu.VMEM((nslot, tn), F32),
                        pltpu.VMEM((nslot, tn), F32),
                        pltpu.VMEM((nslot, tn), F32),
                        pltpu.VMEM((tn, nslot), F32),
                        pltpu.VMEM((tn, nslot), F32),
                        pltpu.VMEM((tn, nslot), F32),
                        pltpu.VMEM((tn * GS_PITCH, PEER_NKEYS), jnp.uint32),
                        pltpu.VMEM((tn * GS_PITCH, PEER_NKEYS), jnp.uint32),
                        pltpu.VMEM((2, tn, te), BF),
                        pltpu.VMEM((tn, D), F32)],
        compiler_params=_cparams(("arbitrary", "arbitrary")),
    )(q2, xn2, keys, ut, v, h2, modtab)


def _pack_kernel(w_ref, o_ref, *, transpose):
    w = w_ref[0].T if transpose else w_ref[0]
    o_ref[0] = pltpu.bitcast(w.astype(BF), jnp.uint32)


def _pack_experts(w, *, transpose):
    nl, ne, d = w.shape
    te = 512
    if transpose:
        out_block, out_map, out_dims = (1, _words(d), te), (lambda l, i: (l, 0, i)), (nl, _words(d), ne)
    else:
        out_block, out_map, out_dims = (1, _words(te), d), (lambda l, i: (l, i, 0)), (nl, _words(ne), d)
    return pl.pallas_call(
        functools.partial(_pack_kernel, transpose=transpose),
        grid=(nl, ne // te),
        in_specs=[pl.BlockSpec((1, te, d), lambda l, i: (l, i, 0))],
        out_specs=pl.BlockSpec(out_block, out_map),
        out_shape=jax.ShapeDtypeStruct(out_dims, jnp.uint32),
        compiler_params=_cparams(("parallel", "parallel")),
    )(w)


def _final_kernel(h_ref, g_ref, o_ref):
    x = h_ref[0]
    o_ref[0] = x * lax.rsqrt(jnp.mean(x * x, axis=-1, keepdims=True) + EPS) * g_ref[...]


def _final_norm(h, g):
    bsz, s, _ = h.shape
    return pl.pallas_call(
        _final_kernel,
        grid=(bsz, s // TM),
        in_specs=[pl.BlockSpec((1, TM, D), lambda b, i: (b, i, 0)),
                  pl.BlockSpec((1, D), lambda b, i: (0, 0))],
        out_specs=pl.BlockSpec((1, TM, D), lambda b, i: (b, i, 0)),
        out_shape=jax.ShapeDtypeStruct((bsz, s, D), F32),
        compiler_params=_cparams(("parallel", "parallel")),
    )(h, g.reshape(1, D))


def _rot_half_cols(w, dh):
    d, n = w.shape
    wh = w.reshape(d, n // dh, 2, dh // 2)
    return jnp.concatenate([-wh[:, :, 1], wh[:, :, 0]], axis=2).reshape(d, n)


def _rope_tables(s, dh, width):
    rows = s // GRID_W
    r = jnp.repeat(jnp.arange(rows, dtype=F32), GRID_W)
    col = jnp.broadcast_to(jnp.arange(GRID_W, dtype=F32), (rows, GRID_W)).reshape(-1)
    nf = dh // 4
    inv = ROPE_THETA ** (-jnp.arange(nf, dtype=F32) / nf)
    ang = jnp.concatenate([r[:, None] * inv, col[:, None] * inv], axis=-1)
    cos = jnp.concatenate([jnp.ones((CTX, dh // 2), F32), jnp.cos(ang)], axis=0)
    sin = jnp.concatenate([jnp.zeros((CTX, dh // 2), F32), jnp.sin(ang)], axis=0)
    reps = width // (dh // 2)
    return jnp.tile(cos, (1, reps)), jnp.tile(sin, (1, reps))


def _interleave_heads(a, nh, dh):
    lead = a.shape[:-1]
    return a.reshape(lead + (2, nh, dh)).swapaxes(-3, -2).reshape(lead + (2 * nh * dh,))


def kernel(x, c, ctx, c_ctx, mod_w, mod_b, norm1_g, norm2_g, ev_w_in, ev_ml_conv_w, ev_ml_conv_b,
           ev_ml_gate_b, ev_ml_norm_g, ev_da_lam, ev_da_norm_g, ev_w_out, od_w_in, od_sink, od_w_out,
           pr_w_q, pr_keys, pr_u, pr_v, final_g):
    bsz, s, _ = x.shape
    depth = mod_w.shape[0]
    t = CTX + s
    ml_w = ML_HEADS * ML_DV
    da_w = DA_HEADS * DA_DV
    qk_w = 2 * ML_HEADS * ML_DQK
    da_qw = DA_HEADS * 2 * DA_DQK

    h = jnp.concatenate([ctx, x], axis=1)

    rows = -(-(bsz + 1) // 8) * 8
    cc = jnp.zeros((rows, D), F32).at[:bsz].set(c).at[bsz].set(c_ctx)
    mods = _modulation(cc, mod_w, mod_b)
    mod_lat = mods[:, :bsz].reshape(depth, bsz, N_MOD, D)
    mod_ctx = jnp.broadcast_to(mods[:, bsz].reshape(depth, 1, N_MOD, D), (depth, bsz, N_MOD, D))
    modtabs = jnp.stack([mod_ctx, mod_lat], axis=2)

    wa_rope_w = (WA_HEADS + WA_KV_HEADS) * WA_DH
    cos_t, sin_t = _rope_tables(s, WA_DH, wa_rope_w)
    ut_words = _pack_experts(pr_u, transpose=True)
    v_words = _pack_experts(pr_v, transpose=False)

    for i in range(depth):
        j = i // 2
        modtab = modtabs[i]
        if i % 2 == 0:
            lam_init = 0.8 - 0.6 * math.exp(-0.3 * i)
            w = ev_w_in[j]
            o0 = 0
            w_mqk = w[:, o0:o0 + qk_w]; o0 += qk_w
            w_mv = w[:, o0:o0 + ml_w]; o0 += ml_w
            w_mo = w[:, o0:o0 + ml_w]; o0 += ml_w
            w_g = w[:, o0:o0 + 4 * ML_HEADS]; o0 += 4 * ML_HEADS
            w_dq = w[:, o0:o0 + da_qw]; o0 += da_qw
            w_dk = w[:, o0:o0 + da_qw]; o0 += da_qw
            w_dv = w[:, o0:o0 + da_w]
            w_rope = jnp.concatenate([w_dq, w_dk], axis=1)
            gpad = jnp.zeros((D, 128 - 4 * ML_HEADS), F32)
            wm = jnp.concatenate([w_rope, _interleave_heads(w_mqk, ML_HEADS, ML_DQK), w_mv, w_mo, w_dv, w_g, gpad],
                                 axis=1).astype(BF)
            wr = _rot_half_cols(w_rope, DA_DQK).astype(BF)
            nr = 2 * da_qw
            col_mqk = nr
            col_mv = col_mqk + qk_w
            col_mo = col_mv + ml_w
            col_dv = col_mo + ml_w
            col_g = col_dv + da_w
            (y,) = _norm_mod_proj(h, modtab, norm1_g[i], wm, wr, cos_t[:, :nr], sin_t[:, :nr],
                                  shift_idx=0, scale_idx=1)
            gates_t = jnp.swapaxes(y[:, :, col_g:col_g + 4 * ML_HEADS], 1, 2)
            gk, qs = _gate_scan(gates_t, ev_ml_gate_b[j])
            qs = qs.reshape(bsz, 4, ML_HEADS, t).transpose(0, 2, 3, 1)
            ml = _mlstm(y, gk, qs, _interleave_heads(ev_ml_conv_w[j], ML_HEADS, ML_DQK),
                        _interleave_heads(ev_ml_conv_b[j], ML_HEADS, ML_DQK), ev_ml_norm_g[j],
                        col_qk=col_mqk, col_v=col_mv, col_o=col_mo)
            da = _diff_attention(y, ev_da_lam[j], ev_da_norm_g[j], col_q=0, col_k=da_qw, col_v=col_dv,
                                 lam_init=lam_init)
            w_out = ev_w_out[j].astype(BF)
            h = _outproj_residual(h, modtab, [(ml, w_out[:ml_w]), (da, w_out[ml_w:])], gate_idx=2)
        else:
            w = od_w_in[j]
            wm = w.astype(BF)
            wr = _rot_half_cols(w[:, :wa_rope_w], WA_DH).astype(BF)
            (y,) = _norm_mod_proj(h, modtab, norm1_g[i], wm, wr, cos_t, sin_t, shift_idx=0, scale_idx=1)
            att = _window_attention(y, od_sink[j], col_q=0, col_k=WA_HEADS * WA_DH,
                                    col_v=WA_HEADS * WA_DH + WA_KV_HEADS * WA_DH)
            h = _outproj_residual(h, modtab, [(att, od_w_out[j].astype(BF))], gate_idx=2)
        ctx_rows = CTX if i < depth - 1 else 0
        hp = h if ctx_rows else h[:, CTX:]
        tp = hp.shape[1]
        q, xn = _norm_mod_proj(hp, modtab, norm2_g[i], pr_w_q[i].astype(BF), shift_idx=3, scale_idx=4,
                               want_xn=True, split=PEER_DHALF, ctx_rows=ctx_rows)
        h = _peer_residual(hp.reshape(bsz * tp, D), q, xn.reshape(-1, D), pr_keys[i].astype(BF),
                           ut_words, v_words, i, modtab,
                           gate_idx=5, tok_per_batch=tp, ctx_rows=ctx_rows).reshape(bsz, tp, D)
    return _final_norm(h, final_g)
```

```python
import functools
import math

import jax
import jax.numpy as jnp
from jax import lax
from jax.experimental import pallas as pl
from jax.experimental.pallas import tpu as pltpu

D = 1024
CTX = 256
GRID_W = 64
EPS = 1e-6
ROPE_THETA = 10000.0
N_MOD = 6

ML_HEADS = 4
ML_DQK = 64
ML_DV = 128
DA_HEADS = 4
DA_DQK = 64
DA_DV = 128
WA_HEADS = 16
WA_KV_HEADS = 4
WA_GROUP = 4
WA_DH = 64
WINDOW = 128

PEER_HEADS = 8
PEER_NKEYS = 128
PEER_DHALF = 128
PEER_TOPK = 16

TM = 256
TQ_ML = 256
TQ_DA = 256
TQ_WA = 128
PEER_TN = 256
PEER_TE = 2048
PEER_CW = 512
GS_PITCH = 72
VMEM_LIMIT = 56 * 1024 * 1024

BF = jnp.bfloat16
F32 = jnp.float32
NEG_INF = float("-inf")


def _cparams(sem):
    return pltpu.CompilerParams(dimension_semantics=sem, vmem_limit_bytes=VMEM_LIMIT)


def _nt_dot(a, b):
    return lax.dot_general(a, b, (((1,), (1,)), ((), ())), preferred_element_type=F32)


def _tn_dot(a, b):
    return lax.dot_general(a, b, (((0,), (0,)), ((), ())), preferred_element_type=F32)


def _dot(a, b):
    return jnp.dot(a, b, preferred_element_type=F32)


def _words(rows):
    return rows * jnp.dtype(BF).itemsize // 4


def _mod_kernel(cc_ref, w_ref, b_ref, o_ref):
    a = jax.nn.silu(cc_ref[...]).astype(BF)
    o_ref[0] = _dot(a, w_ref[0].astype(BF)) + b_ref[0]


def _modulation(cc, mod_w, mod_b):
    depth = mod_w.shape[0]
    rows = cc.shape[0]
    tn = 1536
    return pl.pallas_call(
        _mod_kernel,
        grid=(depth, N_MOD * D // tn),
        in_specs=[pl.BlockSpec((rows, D), lambda l, n: (0, 0)),
                  pl.BlockSpec((1, D, tn), lambda l, n: (l, 0, n)),
                  pl.BlockSpec((1, 1, tn), lambda l, n: (l, 0, n))],
        out_specs=pl.BlockSpec((1, rows, tn), lambda l, n: (l, 0, n)),
        out_shape=jax.ShapeDtypeStruct((depth, rows, N_MOD * D), F32),
        compiler_params=_cparams(("parallel", "parallel")),
    )(cc, mod_w, mod_b.reshape(depth, 1, N_MOD * D))


def _is_latent_tile(tile, ctx_tiles):
    return jnp.minimum(tile // ctx_tiles, 1) if ctx_tiles else 1


def _mod_spec(ctx_rows=CTX):
    return pl.BlockSpec((1, 1, N_MOD, D), lambda b, t: (b, _is_latent_tile(t, ctx_rows // TM), 0, 0))


def _proj_kernel(*refs, shift_idx, scale_idx, nrope, want_xn, split):
    it = iter(refs)
    h_ref, mod_ref, g_ref, wm_ref = next(it), next(it), next(it), next(it)
    if nrope:
        wr_ref, cos_ref, sin_ref = next(it), next(it), next(it)
    y_ref = next(it)
    x = h_ref[0]
    xn = x * lax.rsqrt(jnp.mean(x * x, axis=-1, keepdims=True) + EPS) * g_ref[...]
    mod = mod_ref[0, 0]
    xm = xn * (1.0 + mod[scale_idx:scale_idx + 1]) + mod[shift_idx:shift_idx + 1]
    xb = xm.astype(BF)
    acc = _dot(xb, wm_ref[...])
    if nrope:
        rot = _dot(xb, wr_ref[...])
        y_ref[0, :, :nrope] = acc[:, :nrope] * cos_ref[...] + rot * sin_ref[...]
        y_ref[0, :, nrope:] = acc[:, nrope:]
    elif split:
        for c in range(acc.shape[1] // split):
            y_ref[c] = acc[:, c * split:(c + 1) * split].astype(y_ref.dtype)
    else:
        y_ref[0] = acc
    if want_xn:
        next(it)[0] = pltpu.bitcast(xb, jnp.uint32)


def _norm_mod_proj(h, modtab, g, wm, wr=None, cos=None, sin=None, *, shift_idx, scale_idx, want_xn=False,
                   split=0, ctx_rows=CTX):
    bsz, t, _ = h.shape
    nm = wm.shape[1]
    nrope = 0 if wr is None else wr.shape[1]
    in_specs = [pl.BlockSpec((1, TM, D), lambda b, i: (b, i, 0)),
                _mod_spec(ctx_rows),
                pl.BlockSpec((1, D), lambda b, i: (0, 0)),
                pl.BlockSpec((D, nm), lambda b, i: (0, 0))]
    args = [h, modtab, g.reshape(1, D), wm]
    if nrope:
        in_specs += [pl.BlockSpec((D, nrope), lambda b, i: (0, 0)),
                     pl.BlockSpec((TM, nrope), lambda b, i: (i, 0)),
                     pl.BlockSpec((TM, nrope), lambda b, i: (i, 0))]
        args += [wr, cos, sin]
    out_specs = [pl.BlockSpec((1, TM, nm), lambda b, i: (b, i, 0))]
    out_shape = [jax.ShapeDtypeStruct((bsz, t, nm), F32)]
    if split:
        out_specs = [pl.BlockSpec((nm // split, TM, split), lambda b, i: (0, b * (t // TM) + i, 0))]
        out_shape = [jax.ShapeDtypeStruct((nm // split, bsz * t, split), BF)]
    if want_xn:
        out_specs.append(pl.BlockSpec((1, _words(TM), D), lambda b, i: (b, i, 0)))
        out_shape.append(jax.ShapeDtypeStruct((bsz, _words(t), D), jnp.uint32))
    return pl.pallas_call(
        functools.partial(_proj_kernel, shift_idx=shift_idx, scale_idx=scale_idx, nrope=nrope, want_xn=want_xn,
                          split=split),
        grid=(bsz, t // TM), in_specs=in_specs, out_specs=out_specs, out_shape=out_shape,
        compiler_params=_cparams(("parallel", "parallel")),
    )(*args)


def _outproj_kernel(*refs, gate_idx, n_src):
    h_ref, mod_ref = refs[0], refs[1]
    o_ref = refs[2 + 2 * n_src]
    acc = None
    for s in range(n_src):
        part = _dot(refs[2 + 2 * s][0].astype(BF), refs[3 + 2 * s][...])
        acc = part if acc is None else acc + part
    gate = mod_ref[0, 0][gate_idx:gate_idx + 1]
    o_ref[0] = h_ref[0] + gate * acc


def _outproj_residual(h, modtab, srcs, *, gate_idx):
    bsz, t, _ = h.shape
    in_specs = [pl.BlockSpec((1, TM, D), lambda b, i: (b, i, 0)), _mod_spec()]
    args = [h, modtab]
    for x, w in srcs:
        k = x.shape[-1]
        in_specs += [pl.BlockSpec((1, TM, k), lambda b, i: (b, i, 0)),
                     pl.BlockSpec((k, D), lambda b, i: (0, 0))]
        args += [x, w]
    return pl.pallas_call(
        functools.partial(_outproj_kernel, gate_idx=gate_idx, n_src=len(srcs)),
        grid=(bsz, t // TM), in_specs=in_specs,
        out_specs=pl.BlockSpec((1, TM, D), lambda b, i: (b, i, 0)),
        out_shape=jax.ShapeDtypeStruct((bsz, t, D), F32),
        compiler_params=_cparams(("parallel", "parallel")),
    )(*args)


def _lane_scan(x, op, ident, lane, reverse):
    for k in range(7):
        s = 1 << k
        if reverse:
            sh = pltpu.roll(x, 128 - s, 1)
            x = op(x, jnp.where(lane < 128 - s, sh, ident))
        else:
            sh = pltpu.roll(x, s, 1)
            x = op(x, jnp.where(lane >= s, sh, ident))
    return x


def _gate_scan_kernel(g_ref, b_ref, gk_ref, qs_ref, *, t):
    nb = t // 128
    nh = ML_HEADS
    lane = lax.broadcasted_iota(jnp.int32, (nh, 128), 1)
    pre = g_ref[0] + b_ref[...]
    for d in range(2):
        ig = pre[(2 * d) * nh:(2 * d + 1) * nh]
        lf = jax.nn.log_sigmoid(pre[(2 * d + 1) * nh:(2 * d + 2) * nh])
        if d == 0:
            order = list(range(nb))
        else:
            order = list(range(CTX // 128 - 1, -1, -1)) + list(range(nb - 1, CTX // 128 - 1, -1))
        carry_b = jnp.zeros((nh, 1), F32)
        carry_m = jnp.full((nh, 1), NEG_INF, F32)
        edge = 0 if d else 127
        for blk in order:
            sl = slice(blk * 128, (blk + 1) * 128)
            bb = _lane_scan(lf[:, sl], jnp.add, 0.0, lane, bool(d)) + carry_b
            gg = ig[:, sl] - bb
            mm = jnp.maximum(_lane_scan(gg, jnp.maximum, NEG_INF, lane, bool(d)), carry_m)
            carry_b = bb[:, edge:edge + 1]
            carry_m = mm[:, edge:edge + 1]
            gk_ref[0, d * nh:(d + 1) * nh, sl] = gg
            qs_ref[0, (2 * d) * nh:(2 * d + 1) * nh, sl] = mm
            qs_ref[0, (2 * d + 1) * nh:(2 * d + 2) * nh, sl] = bb + mm


def _gate_scan(gates_t, gate_b):
    bsz, ng, t = gates_t.shape
    return pl.pallas_call(
        functools.partial(_gate_scan_kernel, t=t),
        grid=(bsz,),
        in_specs=[pl.BlockSpec((1, ng, t), lambda b: (b, 0, 0)),
                  pl.BlockSpec((ng, 1), lambda b: (0, 0))],
        out_specs=[pl.BlockSpec((1, 2 * ML_HEADS, t), lambda b: (b, 0, 0)),
                   pl.BlockSpec((1, 4 * ML_HEADS, t), lambda b: (b, 0, 0))],
        out_shape=[jax.ShapeDtypeStruct((bsz, 2 * ML_HEADS, t), F32),
                   jax.ShapeDtypeStruct((bsz, 4 * ML_HEADS, t), F32)],
        compiler_params=_cparams(("parallel",)),
    )(gates_t, gate_b.reshape(ng, 1))


def _mlstm_kernel(qk_ref, v_ref, o_ref, gk_ref, qs_ref, cw_ref, cb_ref, ng_ref, out_ref,
                  q_s, k_s, v_s, *, t):
    hd = pl.program_id(1)
    x = qk_ref[0]
    row = lax.broadcasted_iota(jnp.int32, (t, 1), 0)
    xp = jnp.where((row == 0) | (row == CTX), 0.0, pltpu.roll(x, 1, 0))
    xn = jnp.where((row == CTX - 1) | (row == t - 1), 0.0, pltpu.roll(x, t - 1, 0))
    cw = cw_ref[...]
    y = cb_ref[...] + xp * cw[0:1] + x * cw[1:2] + xn * cw[2:3]
    y = jax.nn.silu(y)
    q_s[...] = y[:, :ML_DQK].astype(BF)
    k_s[...] = (y[:, ML_DQK:] * (ML_DQK ** -0.5)).astype(BF)
    v_s[...] = v_ref[0].astype(BF)
    lower = lax.broadcasted_iota(jnp.int32, (1, TQ_ML), 1) <= lax.broadcasted_iota(jnp.int32, (TQ_ML, 1), 0)
    upper = lax.broadcasted_iota(jnp.int32, (1, TQ_ML), 1) >= lax.broadcasted_iota(jnp.int32, (TQ_ML, 1), 0)

    for qi in range(t // TQ_ML):
        r0 = qi * TQ_ML
        rows = slice(r0, r0 + TQ_ML)
        s = _nt_dot(q_s[rows, :], k_s[...])
        qs = qs_ref[0, 0, rows, :]
        fwd = ([(0, r0, None)] if r0 else []) + [(r0, r0 + TQ_ML, lower)]
        seg_end = CTX if r0 < CTX else t
        bwd = ([(0, CTX, None)] if r0 >= CTX else []) + [(r0, r0 + TQ_ML, upper)]
        bwd += [(r0 + TQ_ML, seg_end, None)] if r0 + TQ_ML < seg_end else []
        hsum = None
        for d, pieces in enumerate((fwd, bwd)):
            gk = gk_ref[0, pl.ds(d * ML_HEADS + hd, 1), :]
            den = num = None
            for a, b, tri in pieces:
                z = gk[:, a:b] - qs[:, 2 * d:2 * d + 1]
                p = s[:, a:b] * jnp.exp(z if tri is None else jnp.where(tri, z, NEG_INF))
                dp = jnp.sum(p, axis=-1, keepdims=True)
                np_ = _dot(p.astype(BF), v_s[a:b, :])
                den = dp if den is None else den + dp
                num = np_ if num is None else num + np_
            hdir = num / jnp.maximum(jnp.abs(den), jnp.exp(-qs[:, 2 * d + 1:2 * d + 2]))
            hsum = hdir if hsum is None else hsum + hdir
        yn = hsum * lax.rsqrt(jnp.mean(hsum * hsum, axis=-1, keepdims=True) + EPS) * ng_ref[...]
        out_ref[0, rows, :] = yn * jax.nn.sigmoid(o_ref[0, rows, :])


def _mlstm(y, gk, qs, conv_w, conv_b, norm_g, *, col_qk, col_v, col_o):
    bsz, t, _ = y.shape
    w = ML_HEADS * ML_DV
    return pl.pallas_call(
        functools.partial(_mlstm_kernel, t=t),
        grid=(bsz, ML_HEADS),
        in_specs=[pl.BlockSpec((1, t, 128), lambda b, h: (b, 0, col_qk // 128 + h)),
                  pl.BlockSpec((1, t, 128), lambda b, h: (b, 0, col_v // 128 + h)),
                  pl.BlockSpec((1, t, 128), lambda b, h: (b, 0, col_o // 128 + h)),
                  pl.BlockSpec((1, 2 * ML_HEADS, t), lambda b, h: (b, 0, 0)),
                  pl.BlockSpec((1, 1, t, 4), lambda b, h: (b, h, 0, 0)),
                  pl.BlockSpec((3, 128), lambda b, h: (0, h)),
                  pl.BlockSpec((1, 128), lambda b, h: (0, h)),
                  pl.BlockSpec((1, 128), lambda b, h: (0, h))],
        out_specs=pl.BlockSpec((1, t, 128), lambda b, h: (b, 0, h)),
        out_shape=jax.ShapeDtypeStruct((bsz, t, w), F32),
        scratch_shapes=[pltpu.VMEM((t, ML_DQK), BF), pltpu.VMEM((t, ML_DQK), BF), pltpu.VMEM((t, ML_DV), BF)],
        compiler_params=_cparams(("parallel", "parallel")),
    )(y, y, y, gk, qs, conv_w, conv_b.reshape(1, w), norm_g.reshape(1, w))


def _da_kernel(q_ref, k_ref, v_ref, lam_ref, ng_ref, out_ref, *, t, lam_init):
    qt = pl.program_id(2)
    lp = lam_ref[...]
    lam = (jnp.exp(jnp.sum(lp[0:1] * lp[1:2], axis=-1, keepdims=True))
           - jnp.exp(jnp.sum(lp[2:3] * lp[3:4], axis=-1, keepdims=True)) + lam_init)
    q = q_ref[0] * (DA_DQK ** -0.5)

    def attend(nk):
        v = v_ref[0, :nk, :].astype(BF)
        outs = []
        for m in range(2):
            qm = q[:, m * DA_DQK:(m + 1) * DA_DQK].astype(BF)
            km = k_ref[0, :nk, m * DA_DQK:(m + 1) * DA_DQK].astype(BF)
            s = _nt_dot(qm, km)
            e = jnp.exp(s - jnp.max(s, axis=-1, keepdims=True))
            outs.append(_dot(e.astype(BF), v) / jnp.sum(e, axis=-1, keepdims=True))
        o = outs[0] - lam * outs[1]
        yn = o * lax.rsqrt(jnp.mean(o * o, axis=-1, keepdims=True) + EPS) * ng_ref[...]
        out_ref[0] = yn * (1.0 - lam_init)

    @pl.when(qt < CTX // TQ_DA)
    def _():
        attend(CTX)

    @pl.when(qt >= CTX // TQ_DA)
    def _():
        attend(t)


def _diff_attention(y, da_lam, norm_g, *, col_q, col_k, col_v, lam_init):
    bsz, t, _ = y.shape
    w = DA_HEADS * DA_DV
    return pl.pallas_call(
        functools.partial(_da_kernel, t=t, lam_init=lam_init),
        grid=(bsz, DA_HEADS, t // TQ_DA),
        in_specs=[pl.BlockSpec((1, TQ_DA, 128), lambda b, h, i: (b, i, col_q // 128 + h)),
                  pl.BlockSpec((1, t, 128), lambda b, h, i: (b, 0, col_k // 128 + h)),
                  pl.BlockSpec((1, t, 128), lambda b, h, i: (b, 0, col_v // 128 + h)),
                  pl.BlockSpec((4, DA_DQK), lambda b, h, i: (0, 0)),
                  pl.BlockSpec((1, 128), lambda b, h, i: (0, h))],
        out_specs=pl.BlockSpec((1, TQ_DA, 128), lambda b, h, i: (b, i, h)),
        out_shape=jax.ShapeDtypeStruct((bsz, t, w), F32),
        compiler_params=_cparams(("parallel", "parallel", "parallel")),
    )(y, y, y, da_lam, norm_g.reshape(1, w))


def _wa_kernel(q_ref, k_ref, v_ref, sink_ref, out_ref, *, t):
    i = pl.program_id(1)
    n_ctx_blk = CTX // TQ_WA
    n_lat = t - CTX
    span = 3 * WINDOW

    def attend(local):
        if local:
            lat0 = jnp.clip((i - n_ctx_blk - 1) * TQ_WA, 0, n_lat - span)
            start = pl.multiple_of(CTX + lat0, TQ_WA)
            qcol = lax.broadcasted_iota(jnp.int32, (1, WA_GROUP, TQ_WA), 2).reshape(1, WA_GROUP * TQ_WA)
            kpos = lat0 + lax.broadcasted_iota(jnp.int32, (span, 1), 0)
            mask = jnp.abs((i - n_ctx_blk) * TQ_WA + qcol - kpos) <= WINDOW
        for g in range(WA_KV_HEADS):
            cs = slice(g * WA_DH, (g + 1) * WA_DH)
            heads = range(g * WA_GROUP, (g + 1) * WA_GROUP)
            kc = k_ref[0, :CTX, cs].astype(BF)
            vc = v_ref[0, :CTX, cs].astype(BF)
            q = jnp.concatenate([q_ref[0, :, hd * WA_DH:(hd + 1) * WA_DH] for hd in heads], axis=0)
            q = (q * (WA_DH ** -0.5)).astype(BF)
            sink = jnp.concatenate([jnp.broadcast_to(sink_ref[:, hd:hd + 1], (1, TQ_WA)) for hd in heads], axis=1)
            sc = _nt_dot(kc, q)
            mx = jnp.maximum(jnp.max(sc, axis=0, keepdims=True), sink)
            if local:
                kl = k_ref[0, pl.ds(start, span), cs].astype(BF)
                vl = v_ref[0, pl.ds(start, span), cs].astype(BF)
                sl = jnp.where(mask, _nt_dot(kl, q), NEG_INF)
                mx = jnp.maximum(mx, jnp.max(sl, axis=0, keepdims=True))
            ec = jnp.exp(sc - mx)
            den = jnp.sum(ec, axis=0, keepdims=True) + jnp.exp(sink - mx)
            num = _tn_dot(vc, ec.astype(BF))
            if local:
                el = jnp.exp(sl - mx)
                den = den + jnp.sum(el, axis=0, keepdims=True)
                num = num + _tn_dot(vl, el.astype(BF))
            o = (num / den).T
            out_ref[0, :, g * WA_GROUP * WA_DH:(g + 1) * WA_GROUP * WA_DH] = jnp.concatenate(
                [o[hh * TQ_WA:(hh + 1) * TQ_WA] for hh in range(WA_GROUP)], axis=1)

    @pl.when(i < n_ctx_blk)
    def _():
        attend(False)

    @pl.when(i >= n_ctx_blk)
    def _():
        attend(True)


def _window_attention(y, sink, *, col_q, col_k, col_v):
    bsz, t, _ = y.shape
    qw = WA_HEADS * WA_DH
    kw = WA_KV_HEADS * WA_DH
    return pl.pallas_call(
        functools.partial(_wa_kernel, t=t),
        grid=(bsz, t // TQ_WA),
        in_specs=[pl.BlockSpec((1, TQ_WA, qw), lambda b, i: (b, i, col_q // qw)),
                  pl.BlockSpec((1, t, kw), lambda b, i: (b, 0, col_k // kw)),
                  pl.BlockSpec((1, t, kw), lambda b, i: (b, 0, col_v // kw)),
                  pl.BlockSpec((1, WA_HEADS), lambda b, i: (0, 0))],
        out_specs=pl.BlockSpec((1, TQ_WA, qw), lambda b, i: (b, i, 0)),
        out_shape=jax.ShapeDtypeStruct((bsz, t, qw), F32),
        compiler_params=_cparams(("parallel", "parallel")),
    )(y, y, y, sink.reshape(1, WA_HEADS))


def _pop_max(x, rowid):
    m = jnp.max(x, axis=0, keepdims=True)
    first = jnp.min(jnp.where(x == m, rowid, float(x.shape[0])), axis=0, keepdims=True)
    hit = rowid == first
    return m, first, hit, jnp.where(hit, NEG_INF, x)


def _top_values(x, k):
    half = x.shape[0] // 2
    top, bot = x[:half], x[half:]
    idx = lax.broadcasted_iota(jnp.int32, top.shape, 0).astype(F32)
    swap = bot > top
    hi_v, lo_v = jnp.maximum(top, bot), jnp.minimum(top, bot)
    hi_i, lo_i = jnp.where(swap, idx + half, idx), jnp.where(swap, idx, idx + half)
    for _ in range(k):
        m = jnp.max(hi_v, axis=0, keepdims=True)
        first = jnp.min(jnp.where(hi_v == m, hi_i, float(2 * half)), axis=0, keepdims=True)
        hit = hi_i == first
        hi_v, hi_i = jnp.where(hit, lo_v, hi_v), jnp.where(hit, lo_i, hi_i)
        lo_v = jnp.where(hit, NEG_INF, lo_v)
        yield m, first


_CAND = [(a, b) for a in range(PEER_TOPK) for b in range(PEER_TOPK) if (a + 1) * (b + 1) <= PEER_TOPK]


def _peer_kernel(q_ref, xn_ref, keys_ref, ut_ref, v_ref, h_ref, mod_ref, out_ref,
                 i1_s, i2_s, gt_s, i1t_s, i2t_s, gtt_s, gsa_s, gsb_s, w_s, acc_s, *, gate_idx):
    r = pl.program_id(0)
    e = pl.program_id(1)
    tn = h_ref.shape[0]
    nslot = PEER_HEADS * PEER_TOPK

    @pl.when((e == 0) & (r == 0))
    def _init():
        i1_s[...] = jnp.zeros_like(i1_s)
        i2_s[...] = jnp.zeros_like(i2_s)
        gt_s[...] = jnp.zeros_like(gt_s)
        gsa_s[...] = jnp.zeros_like(gsa_s)
        gsb_s[...] = jnp.zeros_like(gsb_s)
        w_s[...] = jnp.zeros_like(w_s)
        acc_s[...] = jnp.zeros_like(acc_s)

    @pl.when(e == 0)
    def _slots():
        i1t_s[...] = i1_s[...].T
        i2t_s[...] = i2_s[...].T
        gtt_s[...] = gt_s[...].T

    def route(tok):
        nt = tok.stop - tok.start
        crow = lax.broadcasted_iota(jnp.int32, (len(_CAND), nt), 0).astype(F32)
        tops, idxs = [], []
        for p in range(2):
            vals, ids = [], []
            for m, first in _top_values(_nt_dot(keys_ref[0, p], q_ref[p, tok, :]), PEER_TOPK):
                vals.append(m)
                ids.append(first)
                yield
            tops.append(vals)
            idxs.append(ids)
        work = jnp.concatenate([tops[0][a] + tops[1][b] for a, b in _CAND], axis=0)
        ci1 = jnp.concatenate([idxs[0][a] for a, _ in _CAND], axis=0)
        ci2 = jnp.concatenate([idxs[1][b] for _, b in _CAND], axis=0)
        sv, s1, s2 = [], [], []
        for _ in range(PEER_TOPK):
            m, _, hit, work = _pop_max(work, crow)
            sv.append(m)
            s1.append(jnp.max(jnp.where(hit, ci1, -1.0), axis=0, keepdims=True))
            s2.append(jnp.max(jnp.where(hit, ci2, -1.0), axis=0, keepdims=True))
            yield
        ev = jnp.exp(jnp.concatenate(sv, axis=0) - sv[0])
        rows = pl.ds(pl.multiple_of(e * PEER_TOPK, PEER_TOPK), PEER_TOPK)
        i1_s[rows, tok] = jnp.concatenate(s1, axis=0)
        i2_s[rows, tok] = jnp.concatenate(s2, axis=0)
        gt_s[rows, tok] = ev * (0.5 / jnp.sum(ev, axis=0, keepdims=True))

    def build_gates(gs_write):
        sub = lax.broadcasted_iota(jnp.int32, (PEER_NKEYS, nslot), 0).astype(F32)
        per_step = tn // PEER_HEADS
        for j in range(per_step):
            n = e * per_step + j
            at = jnp.where(sub == i1t_s[pl.ds(n, 1), :], gtt_s[pl.ds(n, 1), :], 0.0).astype(BF)
            bt = jnp.where(sub == i2t_s[pl.ds(n, 1), :], 1.0, 0.0).astype(BF)
            g = pltpu.bitcast(_nt_dot(at, bt).astype(BF), jnp.uint32)
            gs_write[pl.ds(pl.multiple_of(n * GS_PITCH, 8), _words(PEER_NKEYS)), :] = g
            yield

    def step(gs_read, gs_write):
        routers = [route(slice(k * 128, (k + 1) * 128)) for k in range(tn // 128)]
        gater = build_gates(gs_write)
        n_chunk = PEER_TE // PEER_CW
        pops_per_chunk = 3 * PEER_TOPK // n_chunk
        gates_per_chunk = tn // PEER_HEADS // n_chunk
        per_chunk = PEER_CW // PEER_NKEYS
        chunks_per_out = n_chunk * PEER_CW // D
        cur = e % 2
        xn = pltpu.bitcast(xn_ref[...], BF)
        for router in routers:
            next(router)
        for c in range(n_chunk):
            if c % chunks_per_out == 0:
                cols = slice(c // chunks_per_out * PEER_CW, (c // chunks_per_out + 1) * PEER_CW)
                acc_s[:, cols] += _dot(w_s[1 - cur], pltpu.bitcast(v_ref[:, cols], BF))
            a = _dot(xn, pltpu.bitcast(ut_ref[:, c * PEER_CW:(c + 1) * PEER_CW], BF))
            for _ in range(pops_per_chunk):
                for router in routers:
                    next(router, None)
            for _ in range(gates_per_chunk):
                next(gater, None)
            ws = []
            for k in range(0, per_chunk, 2):
                pair = (e * (PEER_TE // PEER_NKEYS) + c * per_chunk + k) // 2
                words = gs_read[pl.ds(pair, tn, stride=GS_PITCH), :]
                for half, g in enumerate((pltpu.bitcast(words << 16, F32),
                                          pltpu.bitcast(words & jnp.uint32(0xFFFF0000), F32))):
                    ak = a[:, (k + half) * PEER_NKEYS:(k + half + 1) * PEER_NKEYS]
                    ws.append((g * (ak * (1.0 + lax.erf(ak * (0.5 ** 0.5))))).astype(BF))
            w_s[cur, :, c * PEER_CW:(c + 1) * PEER_CW] = jnp.concatenate(ws, axis=1)
        for gen in routers + [gater]:
            for _ in gen:
                pass

    @pl.when(r % 2 == 0)
    def _even():
        step(gsa_s, gsb_s)

    @pl.when(r % 2 == 1)
    def _odd():
        step(gsb_s, gsa_s)

    @pl.when(e == 0)
    def _fin():
        gate = mod_ref[0, 0][gate_idx:gate_idx + 1]
        out_ref[...] = h_ref[...] + gate * acc_s[...]
        acc_s[...] = jnp.zeros_like(acc_s)


def _peer_residual(h2, q2, xn2, keys, ut, v, layer, modtab, *, gate_idx, tok_per_batch, ctx_rows):
    ntok = h2.shape[0]
    nexp = ut.shape[2]
    nslot = PEER_HEADS * PEER_TOPK
    tn, te = PEER_TN, PEER_TE
    ntiles = ntok // tn
    tiles_per_batch = tok_per_batch // tn
    assert nexp // te == PEER_HEADS

    nsteps = nexp // te

    def lagged(r, lag):
        return jnp.clip(r - lag, 0, ntiles - 1)

    def mod_map(r, e):
        t = lagged(r, 3)
        return (t // tiles_per_batch, _is_latent_tile(t % tiles_per_batch, ctx_rows // tn), 0, 0)

    return pl.pallas_call(
        functools.partial(_peer_kernel, gate_idx=gate_idx),
        grid=(ntiles + 3, nsteps),
        in_specs=[pl.BlockSpec((2, tn, PEER_DHALF), lambda r, e: (e, lagged(r, 0), 0)),
                  pl.BlockSpec((_words(tn), D), lambda r, e: (lagged(r, 2), 0)),
                  pl.BlockSpec((1, 2, PEER_NKEYS, PEER_DHALF), lambda r, e: (e, 0, 0, 0)),
                  pl.BlockSpec((None, _words(D), te), lambda r, e: (layer, 0, e)),
                  pl.BlockSpec((None, _words(te), D), lambda r, e: (layer, (e + nsteps - 1) % nsteps, 0)),
                  pl.BlockSpec((tn, D), lambda r, e: (lagged(r, 3), 0)),
                  pl.BlockSpec((1, 1, N_MOD, D), mod_map)],
        out_specs=pl.BlockSpec((tn, D), lambda r, e: (lagged(r, 3), 0)),
        out_shape=jax.ShapeDtypeStruct((ntok, D), F32),
        scratch_shapes=[pltpu.VMEM((nslot, tn), F32),
                        pltpu.VMEM((nslot, tn), F32),
                        pltpu.VMEM((nslot, tn), F32),
                        pltpu.VMEM((tn, nslot), F32),
                        pltpu.VMEM((tn, nslot), F32),
                        pltpu.VMEM((tn, nslot), F32),
                        pltpu.VMEM((tn * GS_PITCH, PEER_NKEYS), jnp.uint32),
                        pltpu.VMEM((tn * GS_PITCH, PEER_NKEYS), jnp.uint32),
                        pltpu.VMEM((2, tn, te), BF),
                        pltpu.VMEM((tn, D), F32)],
        compiler_params=_cparams(("arbitrary", "arbitrary")),
    )(q2, xn2, keys, ut, v, h2, modtab)


def _pack_kernel(w_ref, o_ref, *, transpose):
    w = w_ref[0].T if transpose else w_ref[0]
    o_ref[0] = pltpu.bitcast(w.astype(BF), jnp.uint32)


def _pack_experts(w, *, transpose):
    nl, ne, d = w.shape
    te = 512
    if transpose:
        out_block, out_map, out_dims = (1, _words(d), te), (lambda l, i: (l, 0, i)), (nl, _words(d), ne)
    else:
        out_block, out_map, out_dims = (1, _words(te), d), (lambda l, i: (l, i, 0)), (nl, _words(ne), d)
    return pl.pallas_call(
        functools.partial(_pack_kernel, transpose=transpose),
        grid=(nl, ne // te),
        in_specs=[pl.BlockSpec((1, te, d), lambda l, i: (l, i, 0))],
        out_specs=pl.BlockSpec(out_block, out_map),
        out_shape=jax.ShapeDtypeStruct(out_dims, jnp.uint32),
        compiler_params=_cparams(("parallel", "parallel")),
    )(w)


def _final_kernel(h_ref, g_ref, o_ref):
    x = h_ref[0]
    o_ref[0] = x * lax.rsqrt(jnp.mean(x * x, axis=-1, keepdims=True) + EPS) * g_ref[...]


def _final_norm(h, g):
    bsz, s, _ = h.shape
    return pl.pallas_call(
        _final_kernel,
        grid=(bsz, s // TM),
        in_specs=[pl.BlockSpec((1, TM, D), lambda b, i: (b, i, 0)),
                  pl.BlockSpec((1, D), lambda b, i: (0, 0))],
        out_specs=pl.BlockSpec((1, TM, D), lambda b, i: (b, i, 0)),
        out_shape=jax.ShapeDtypeStruct((bsz, s, D), F32),
        compiler_params=_cparams(("parallel", "parallel")),
    )(h, g.reshape(1, D))


def _rot_half_cols(w, dh):
    d, n = w.shape
    wh = w.reshape(d, n // dh, 2, dh // 2)
    return jnp.concatenate([-wh[:, :, 1], wh[:, :, 0]], axis=2).reshape(d, n)


def _rope_tables(s, dh, width):
    rows = s // GRID_W
    r = jnp.repeat(jnp.arange(rows, dtype=F32), GRID_W)
    col = jnp.broadcast_to(jnp.arange(GRID_W, dtype=F32), (rows, GRID_W)).reshape(-1)
    nf = dh // 4
    inv = ROPE_THETA ** (-jnp.arange(nf, dtype=F32) / nf)
    ang = jnp.concatenate([r[:, None] * inv, col[:, None] * inv], axis=-1)
    cos = jnp.concatenate([jnp.ones((CTX, dh // 2), F32), jnp.cos(ang)], axis=0)
    sin = jnp.concatenate([jnp.zeros((CTX, dh // 2), F32), jnp.sin(ang)], axis=0)
    reps = width // (dh // 2)
    return jnp.tile(cos, (1, reps)), jnp.tile(sin, (1, reps))


def _interleave_heads(a, nh, dh):
    lead = a.shape[:-1]
    return a.reshape(lead + (2, nh, dh)).swapaxes(-3, -2).reshape(lead + (2 * nh * dh,))


def kernel(x, c, ctx, c_ctx, mod_w, mod_b, norm1_g, norm2_g, ev_w_in, ev_ml_conv_w, ev_ml_conv_b,
           ev_ml_gate_b, ev_ml_norm_g, ev_da_lam, ev_da_norm_g, ev_w_out, od_w_in, od_sink, od_w_out,
           pr_w_q, pr_keys, pr_u, pr_v, final_g):
    bsz, s, _ = x.shape
    depth = mod_w.shape[0]
    t = CTX + s
    ml_w = ML_HEADS * ML_DV
    da_w = DA_HEADS * DA_DV
    qk_w = 2 * ML_HEADS * ML_DQK
    da_qw = DA_HEADS * 2 * DA_DQK

    h = jnp.concatenate([ctx, x], axis=1)

    rows = -(-(bsz + 1) // 8) * 8
    cc = jnp.zeros((rows, D), F32).at[:bsz].set(c).at[bsz].set(c_ctx)
    mods = _modulation(cc, mod_w, mod_b)
    mod_lat = mods[:, :bsz].reshape(depth, bsz, N_MOD, D)
    mod_ctx = jnp.broadcast_to(mods[:, bsz].reshape(depth, 1, N_MOD, D), (depth, bsz, N_MOD, D))
    modtabs = jnp.stack([mod_ctx, mod_lat], axis=2)

    wa_rope_w = (WA_HEADS + WA_KV_HEADS) * WA_DH
    cos_t, sin_t = _rope_tables(s, WA_DH, wa_rope_w)
    ut_words = _pack_experts(pr_u, transpose=True)
    v_words = _pack_experts(pr_v, transpose=False)

    for i in range(depth):
        j = i // 2
        modtab = modtabs[i]
        if i % 2 == 0:
            lam_init = 0.8 - 0.6 * math.exp(-0.3 * i)
            w = ev_w_in[j]
            o0 = 0
            w_mqk = w[:, o0:o0 + qk_w]; o0 += qk_w
            w_mv = w[:, o0:o0 + ml_w]; o0 += ml_w
            w_mo = w[:, o0:o0 + ml_w]; o0 += ml_w
            w_g = w[:, o0:o0 + 4 * ML_HEADS]; o0 += 4 * ML_HEADS
            w_dq = w[:, o0:o0 + da_qw]; o0 += da_qw
            w_dk = w[:, o0:o0 + da_qw]; o0 += da_qw
            w_dv = w[:, o0:o0 + da_w]
            w_rope = jnp.concatenate([w_dq, w_dk], axis=1)
            gpad = jnp.zeros((D, 128 - 4 * ML_HEADS), F32)
            wm = jnp.concatenate([w_rope, _interleave_heads(w_mqk, ML_HEADS, ML_DQK), w_mv, w_mo, w_dv, w_g, gpad],
                                 axis=1).astype(BF)
            wr = _rot_half_cols(w_rope, DA_DQK).astype(BF)
            nr = 2 * da_qw
            col_mqk = nr
            col_mv = col_mqk + qk_w
            col_mo = col_mv + ml_w
            col_dv = col_mo + ml_w
            col_g = col_dv + da_w
            (y,) = _norm_mod_proj(h, modtab, norm1_g[i], wm, wr, cos_t[:, :nr], sin_t[:, :nr],
                                  shift_idx=0, scale_idx=1)
            gates_t = jnp.swapaxes(y[:, :, col_g:col_g + 4 * ML_HEADS], 1, 2)
            gk, qs = _gate_scan(gates_t, ev_ml_gate_b[j])
            qs = qs.reshape(bsz, 4, ML_HEADS, t).transpose(0, 2, 3, 1)
            ml = _mlstm(y, gk, qs, _interleave_heads(ev_ml_conv_w[j], ML_HEADS, ML_DQK),
                        _interleave_heads(ev_ml_conv_b[j], ML_HEADS, ML_DQK), ev_ml_norm_g[j],
                        col_qk=col_mqk, col_v=col_mv, col_o=col_mo)
            da = _diff_attention(y, ev_da_lam[j], ev_da_norm_g[j], col_q=0, col_k=da_qw, col_v=col_dv,
                                 lam_init=lam_init)
            w_out = ev_w_out[j].astype(BF)
            h = _outproj_residual(h, modtab, [(ml, w_out[:ml_w]), (da, w_out[ml_w:])], gate_idx=2)
        else:
            w = od_w_in[j]
            wm = w.astype(BF)
            wr = _rot_half_cols(w[:, :wa_rope_w], WA_DH).astype(BF)
            (y,) = _norm_mod_proj(h, modtab, norm1_g[i], wm, wr, cos_t, sin_t, shift_idx=0, scale_idx=1)
            att = _window_attention(y, od_sink[j], col_q=0, col_k=WA_HEADS * WA_DH,
                                    col_v=WA_HEADS * WA_DH + WA_KV_HEADS * WA_DH)
            h = _outproj_residual(h, modtab, [(att, od_w_out[j].astype(BF))], gate_idx=2)
        ctx_rows = CTX if i < depth - 1 else 0
        hp = h if ctx_rows else h[:, CTX:]
        tp = hp.shape[1]
        q, xn = _norm_mod_proj(hp, modtab, norm2_g[i], pr_w_q[i].astype(BF), shift_idx=3, scale_idx=4,
                               want_xn=True, split=PEER_DHALF, ctx_rows=ctx_rows)
        h = _peer_residual(hp.reshape(bsz * tp, D), q, xn.reshape(-1, D), pr_keys[i].astype(BF),
                           ut_words, v_words, i, modtab,
                           gate_idx=5, tok_per_batch=tp, ctx_rows=ctx_rows).reshape(bsz, tp, D)
    return _final_norm(h, final_g)
```

```python
import functools
import math

import jax
import jax.numpy as jnp
from jax import lax
from jax.experimental import pallas as pl
from jax.experimental.pallas import tpu as pltpu

D = 1024
CTX = 256
GRID_W = 64
EPS = 1e-6
ROPE_THETA = 10000.0
N_MOD = 6

ML_HEADS = 4
ML_DQK = 64
ML_DV = 128
DA_HEADS = 4
DA_DQK = 64
DA_DV = 128
WA_HEADS = 16
WA_KV_HEADS = 4
WA_GROUP = 4
WA_DH = 64
WINDOW = 128

PEER_HEADS = 8
PEER_NKEYS = 128
PEER_DHALF = 128
PEER_TOPK = 16

TM = 256
TQ_ML = 256
TQ_DA = 256
TQ_WA = 128
PEER_TN = 256
PEER_TE = 2048
PEER_CW = 512
GS_PITCH = 72
VMEM_LIMIT = 56 * 1024 * 1024

BF = jnp.bfloat16
F32 = jnp.float32
NEG_INF = float("-inf")


def _cparams(sem):
    return pltpu.CompilerParams(dimension_semantics=sem, vmem_limit_bytes=VMEM_LIMIT)


def _nt_dot(a, b):
    return lax.dot_general(a, b, (((1,), (1,)), ((), ())), preferred_element_type=F32)


def _tn_dot(a, b):
    return lax.dot_general(a, b, (((0,), (0,)), ((), ())), preferred_element_type=F32)


def _dot(a, b):
    return jnp.dot(a, b, preferred_element_type=F32)


def _words(rows):
    return rows * jnp.dtype(BF).itemsize // 4


def _mod_kernel(cc_ref, w_ref, b_ref, o_ref):
    a = jax.nn.silu(cc_ref[...]).astype(BF)
    o_ref[0] = _dot(a, w_ref[0].astype(BF)) + b_ref[0]


def _modulation(cc, mod_w, mod_b):
    depth = mod_w.shape[0]
    rows = cc.shape[0]
    tn = 1536
    return pl.pallas_call(
        _mod_kernel,
        grid=(depth, N_MOD * D // tn),
        in_specs=[pl.BlockSpec((rows, D), lambda l, n: (0, 0)),
                  pl.BlockSpec((1, D, tn), lambda l, n: (l, 0, n)),
                  pl.BlockSpec((1, 1, tn), lambda l, n: (l, 0, n))],
        out_specs=pl.BlockSpec((1, rows, tn), lambda l, n: (l, 0, n)),
        out_shape=jax.ShapeDtypeStruct((depth, rows, N_MOD * D), F32),
        compiler_params=_cparams(("parallel", "parallel")),
    )(cc, mod_w, mod_b.reshape(depth, 1, N_MOD * D))


def _is_latent_tile(tile, ctx_tiles):
    return jnp.minimum(tile // ctx_tiles, 1) if ctx_tiles else 1


def _mod_spec():
    return pl.BlockSpec((1, 2, N_MOD, D), lambda b, t: (b, 0, 0, 0))


def _mod_rows(mod_ref, idx, tm, ctx_rows):
    lat = mod_ref[0, 1][idx:idx + 1]
    if not ctx_rows:
        return lat
    row = pl.program_id(1) * tm + lax.broadcasted_iota(jnp.int32, (tm, 1), 0)
    return jnp.where(row < ctx_rows, mod_ref[0, 0][idx:idx + 1], lat)


def _row_tile(t):
    return next(tm for tm in (768, 512, TM) if t % tm == 0)


def _proj_kernel(*refs, shift_idx, scale_idx, nrope, want_xn, split, ctx_rows):
    it = iter(refs)
    h_ref, mod_ref, g_ref, wm_ref = next(it), next(it), next(it), next(it)
    if nrope:
        wr_ref, cos_ref, sin_ref = next(it), next(it), next(it)
    y_ref = next(it)
    x = h_ref[0]
    tm = x.shape[0]
    xn = x * lax.rsqrt(jnp.mean(x * x, axis=-1, keepdims=True) + EPS) * g_ref[...]
    xm = xn * (1.0 + _mod_rows(mod_ref, scale_idx, tm, ctx_rows)) + _mod_rows(mod_ref, shift_idx, tm, ctx_rows)
    xb = xm.astype(BF)
    acc = _dot(xb, wm_ref[...])
    if nrope:
        rot = _dot(xb, wr_ref[...])
        y_ref[0, :, :nrope] = acc[:, :nrope] * cos_ref[...] + rot * sin_ref[...]
        y_ref[0, :, nrope:] = acc[:, nrope:]
    elif split:
        for c in range(acc.shape[1] // split):
            y_ref[c] = acc[:, c * split:(c + 1) * split].astype(y_ref.dtype)
    else:
        y_ref[0] = acc
    if want_xn:
        next(it)[0] = pltpu.bitcast(xb, jnp.uint32)


def _norm_mod_proj(h, modtab, g, wm, wr=None, cos=None, sin=None, *, shift_idx, scale_idx, want_xn=False,
                   split=0, ctx_rows=CTX, tm=TM):
    bsz, t, _ = h.shape
    nm = wm.shape[1]
    nrope = 0 if wr is None else wr.shape[1]
    in_specs = [pl.BlockSpec((1, tm, D), lambda b, i: (b, i, 0)),
                _mod_spec(),
                pl.BlockSpec((1, D), lambda b, i: (0, 0)),
                pl.BlockSpec((D, nm), lambda b, i: (0, 0))]
    args = [h, modtab, g.reshape(1, D), wm]
    if nrope:
        in_specs += [pl.BlockSpec((D, nrope), lambda b, i: (0, 0)),
                     pl.BlockSpec((tm, nrope), lambda b, i: (i, 0)),
                     pl.BlockSpec((tm, nrope), lambda b, i: (i, 0))]
        args += [wr, cos, sin]
    out_specs = [pl.BlockSpec((1, tm, nm), lambda b, i: (b, i, 0))]
    out_shape = [jax.ShapeDtypeStruct((bsz, t, nm), F32)]
    if split:
        out_specs = [pl.BlockSpec((nm // split, tm, split), lambda b, i: (0, b * (t // tm) + i, 0))]
        out_shape = [jax.ShapeDtypeStruct((nm // split, bsz * t, split), BF)]
    if want_xn:
        out_specs.append(pl.BlockSpec((1, _words(tm), D), lambda b, i: (b, i, 0)))
        out_shape.append(jax.ShapeDtypeStruct((bsz, _words(t), D), jnp.uint32))
    return pl.pallas_call(
        functools.partial(_proj_kernel, shift_idx=shift_idx, scale_idx=scale_idx, nrope=nrope, want_xn=want_xn,
                          split=split, ctx_rows=ctx_rows),
        grid=(bsz, t // tm), in_specs=in_specs, out_specs=out_specs, out_shape=out_shape,
        compiler_params=_cparams(("parallel", "parallel")),
    )(*args)


def _outproj_kernel(*refs, gate_idx, n_src):
    h_ref, mod_ref = refs[0], refs[1]
    o_ref = refs[2 + 2 * n_src]
    acc = None
    for s in range(n_src):
        part = _dot(refs[2 + 2 * s][0].astype(BF), refs[3 + 2 * s][...])
        acc = part if acc is None else acc + part
    o_ref[0] = h_ref[0] + _mod_rows(mod_ref, gate_idx, acc.shape[0], CTX) * acc


def _outproj_residual(h, modtab, srcs, *, gate_idx):
    bsz, t, _ = h.shape
    tm = _row_tile(t)
    in_specs = [pl.BlockSpec((1, tm, D), lambda b, i: (b, i, 0)), _mod_spec()]
    args = [h, modtab]
    for x, w in srcs:
        k = x.shape[-1]
        in_specs += [pl.BlockSpec((1, tm, k), lambda b, i: (b, i, 0)),
                     pl.BlockSpec((k, D), lambda b, i: (0, 0))]
        args += [x, w]
    return pl.pallas_call(
        functools.partial(_outproj_kernel, gate_idx=gate_idx, n_src=len(srcs)),
        grid=(bsz, t // tm), in_specs=in_specs,
        out_specs=pl.BlockSpec((1, tm, D), lambda b, i: (b, i, 0)),
        out_shape=jax.ShapeDtypeStruct((bsz, t, D), F32),
        compiler_params=_cparams(("parallel", "parallel")),
    )(*args)


def _lane_scan(x, op, ident, lane, reverse):
    for k in range(7):
        s = 1 << k
        if reverse:
            sh = pltpu.roll(x, 128 - s, 1)
            x = op(x, jnp.where(lane < 128 - s, sh, ident))
        else:
            sh = pltpu.roll(x, s, 1)
            x = op(x, jnp.where(lane >= s, sh, ident))
    return x


def _gate_scan_kernel(g_ref, b_ref, gk_ref, qs_ref, *, t):
    nb = t // 128
    nh = ML_HEADS
    lane = lax.broadcasted_iota(jnp.int32, (nh, 128), 1)
    pre = g_ref[0] + b_ref[...]
    for d in range(2):
        ig = pre[(2 * d) * nh:(2 * d + 1) * nh]
        lf = jax.nn.log_sigmoid(pre[(2 * d + 1) * nh:(2 * d + 2) * nh])
        if d == 0:
            order = list(range(nb))
        else:
            order = list(range(CTX // 128 - 1, -1, -1)) + list(range(nb - 1, CTX // 128 - 1, -1))
        carry_b = jnp.zeros((nh, 1), F32)
        carry_m = jnp.full((nh, 1), NEG_INF, F32)
        edge = 0 if d else 127
        for blk in order:
            sl = slice(blk * 128, (blk + 1) * 128)
            bb = _lane_scan(lf[:, sl], jnp.add, 0.0, lane, bool(d)) + carry_b
            gg = ig[:, sl] - bb
            mm = jnp.maximum(_lane_scan(gg, jnp.maximum, NEG_INF, lane, bool(d)), carry_m)
            carry_b = bb[:, edge:edge + 1]
            carry_m = mm[:, edge:edge + 1]
            gk_ref[0, d * nh:(d + 1) * nh, sl] = gg
            qs_ref[0, (2 * d) * nh:(2 * d + 1) * nh, sl] = mm
            qs_ref[0, (2 * d + 1) * nh:(2 * d + 2) * nh, sl] = bb + mm


def _gate_scan(gates_t, gate_b):
    bsz, ng, t = gates_t.shape
    return pl.pallas_call(
        functools.partial(_gate_scan_kernel, t=t),
        grid=(bsz,),
        in_specs=[pl.BlockSpec((1, ng, t), lambda b: (b, 0, 0)),
                  pl.BlockSpec((ng, 1), lambda b: (0, 0))],
        out_specs=[pl.BlockSpec((1, 2 * ML_HEADS, t), lambda b: (b, 0, 0)),
                   pl.BlockSpec((1, 4 * ML_HEADS, t), lambda b: (b, 0, 0))],
        out_shape=[jax.ShapeDtypeStruct((bsz, 2 * ML_HEADS, t), F32),
                   jax.ShapeDtypeStruct((bsz, 4 * ML_HEADS, t), F32)],
        compiler_params=_cparams(("parallel",)),
    )(gates_t, gate_b.reshape(ng, 1))


def _mlstm_kernel(qk_ref, v_ref, o_ref, gk_ref, qs_ref, cw_ref, cb_ref, ng_ref, out_ref,
                  q_s, k_s, v_s, *, t):
    hd = pl.program_id(1)
    x = qk_ref[0]
    row = lax.broadcasted_iota(jnp.int32, (t, 1), 0)
    xp = jnp.where((row == 0) | (row == CTX), 0.0, pltpu.roll(x, 1, 0))
    xn = jnp.where((row == CTX - 1) | (row == t - 1), 0.0, pltpu.roll(x, t - 1, 0))
    cw = cw_ref[...]
    y = cb_ref[...] + xp * cw[0:1] + x * cw[1:2] + xn * cw[2:3]
    y = jax.nn.silu(y)
    q_s[...] = y[:, :ML_DQK].astype(BF)
    k_s[...] = (y[:, ML_DQK:] * (ML_DQK ** -0.5)).astype(BF)
    v_s[...] = v_ref[0].astype(BF)
    lower = lax.broadcasted_iota(jnp.int32, (1, TQ_ML), 1) <= lax.broadcasted_iota(jnp.int32, (TQ_ML, 1), 0)
    upper = lax.broadcasted_iota(jnp.int32, (1, TQ_ML), 1) >= lax.broadcasted_iota(jnp.int32, (TQ_ML, 1), 0)

    for qi in range(t // TQ_ML):
        r0 = qi * TQ_ML
        rows = slice(r0, r0 + TQ_ML)
        s = _nt_dot(q_s[rows, :], k_s[...])
        qs = qs_ref[0, 0, rows, :]
        fwd = ([(0, r0, None)] if r0 else []) + [(r0, r0 + TQ_ML, lower)]
        seg_end = CTX if r0 < CTX else t
        bwd = ([(0, CTX, None)] if r0 >= CTX else []) + [(r0, r0 + TQ_ML, upper)]
        bwd += [(r0 + TQ_ML, seg_end, None)] if r0 + TQ_ML < seg_end else []
        hsum = None
        for d, pieces in enumerate((fwd, bwd)):
            gk = gk_ref[0, pl.ds(d * ML_HEADS + hd, 1), :]
            den = num = None
            for a, b, tri in pieces:
                z = gk[:, a:b] - qs[:, 2 * d:2 * d + 1]
                p = s[:, a:b] * jnp.exp(z if tri is None else jnp.where(tri, z, NEG_INF))
                dp = jnp.sum(p, axis=-1, keepdims=True)
                np_ = _dot(p.astype(BF), v_s[a:b, :])
                den = dp if den is None else den + dp
                num = np_ if num is None else num + np_
            hdir = num / jnp.maximum(jnp.abs(den), jnp.exp(-qs[:, 2 * d + 1:2 * d + 2]))
            hsum = hdir if hsum is None else hsum + hdir
        yn = hsum * lax.rsqrt(jnp.mean(hsum * hsum, axis=-1, keepdims=True) + EPS) * ng_ref[...]
        out_ref[0, rows, :] = yn * jax.nn.sigmoid(o_ref[0, rows, :])


def _mlstm(y, gk, qs, conv_w, conv_b, norm_g, *, col_qk, col_v, col_o):
    bsz, t, _ = y.shape
    w = ML_HEADS * ML_DV
    return pl.pallas_call(
        functools.partial(_mlstm_kernel, t=t),
        grid=(bsz, ML_HEADS),
        in_specs=[pl.BlockSpec((1, t, 128), lambda b, h: (b, 0, col_qk // 128 + h)),
                  pl.BlockSpec((1, t, 128), lambda b, h: (b, 0, col_v // 128 + h)),
                  pl.BlockSpec((1, t, 128), lambda b, h: (b, 0, col_o // 128 + h)),
                  pl.BlockSpec((1, 2 * ML_HEADS, t), lambda b, h: (b, 0, 0)),
                  pl.BlockSpec((1, 1, t, 4), lambda b, h: (b, h, 0, 0)),
                  pl.BlockSpec((3, 128), lambda b, h: (0, h)),
                  pl.BlockSpec((1, 128), lambda b, h: (0, h)),
                  pl.BlockSpec((1, 128), lambda b, h: (0, h))],
        out_specs=pl.BlockSpec((1, t, 128), lambda b, h: (b, 0, h)),
        out_shape=jax.ShapeDtypeStruct((bsz, t, w), F32),
        scratch_shapes=[pltpu.VMEM((t, ML_DQK), BF), pltpu.VMEM((t, ML_DQK), BF), pltpu.VMEM((t, ML_DV), BF)],
        compiler_params=_cparams(("parallel", "parallel")),
    )(y, y, y, gk, qs, conv_w, conv_b.reshape(1, w), norm_g.reshape(1, w))


def _da_kernel(q_ref, k_ref, v_ref, lam_ref, ng_ref, out_ref, *, t, lam_init):
    qt = pl.program_id(2)
    lp = lam_ref[...]
    lam = (jnp.exp(jnp.sum(lp[0:1] * lp[1:2], axis=-1, keepdims=True))
           - jnp.exp(jnp.sum(lp[2:3] * lp[3:4], axis=-1, keepdims=True)) + lam_init)
    q = q_ref[0] * (DA_DQK ** -0.5)

    def attend(nk):
        v = v_ref[0, :nk, :].astype(BF)
        outs = []
        for m in range(2):
            qm = q[:, m * DA_DQK:(m + 1) * DA_DQK].astype(BF)
            km = k_ref[0, :nk, m * DA_DQK:(m + 1) * DA_DQK].astype(BF)
            s = _nt_dot(qm, km)
            e = jnp.exp(s - jnp.max(s, axis=-1, keepdims=True))
            outs.append(_dot(e.astype(BF), v) / jnp.sum(e, axis=-1, keepdims=True))
        o = outs[0] - lam * outs[1]
        yn = o * lax.rsqrt(jnp.mean(o * o, axis=-1, keepdims=True) + EPS) * ng_ref[...]
        out_ref[0] = yn * (1.0 - lam_init)

    @pl.when(qt < CTX // TQ_DA)
    def _():
        attend(CTX)

    @pl.when(qt >= CTX // TQ_DA)
    def _():
        attend(t)


def _diff_attention(y, da_lam, norm_g, *, col_q, col_k, col_v, lam_init):
    bsz, t, _ = y.shape
    w = DA_HEADS * DA_DV
    return pl.pallas_call(
        functools.partial(_da_kernel, t=t, lam_init=lam_init),
        grid=(bsz, DA_HEADS, t // TQ_DA),
        in_specs=[pl.BlockSpec((1, TQ_DA, 128), lambda b, h, i: (b, i, col_q // 128 + h)),
                  pl.BlockSpec((1, t, 128), lambda b, h, i: (b, 0, col_k // 128 + h)),
                  pl.BlockSpec((1, t, 128), lambda b, h, i: (b, 0, col_v // 128 + h)),
                  pl.BlockSpec((4, DA_DQK), lambda b, h, i: (0, 0)),
                  pl.BlockSpec((1, 128), lambda b, h, i: (0, h))],
        out_specs=pl.BlockSpec((1, TQ_DA, 128), lambda b, h, i: (b, i, h)),
        out_shape=jax.ShapeDtypeStruct((bsz, t, w), F32),
        compiler_params=_cparams(("parallel", "parallel", "parallel")),
    )(y, y, y, da_lam, norm_g.reshape(1, w))


def _wa_kernel(q_ref, k_ref, v_ref, sink_ref, out_ref, *, t):
    i = pl.program_id(1)
    n_ctx_blk = CTX // TQ_WA
    n_lat = t - CTX
    span = 3 * WINDOW

    def attend(local):
        if local:
            lat0 = jnp.clip((i - n_ctx_blk - 1) * TQ_WA, 0, n_lat - span)
            start = pl.multiple_of(CTX + lat0, TQ_WA)
            qcol = lax.broadcasted_iota(jnp.int32, (1, WA_GROUP, TQ_WA), 2).reshape(1, WA_GROUP * TQ_WA)
            kpos = lat0 + lax.broadcasted_iota(jnp.int32, (span, 1), 0)
            mask = jnp.abs((i - n_ctx_blk) * TQ_WA + qcol - kpos) <= WINDOW
        for g in range(WA_KV_HEADS):
            cs = slice(g * WA_DH, (g + 1) * WA_DH)
            heads = range(g * WA_GROUP, (g + 1) * WA_GROUP)
            kc = k_ref[0, :CTX, cs].astype(BF)
            vc = v_ref[0, :CTX, cs].astype(BF)
            q = jnp.concatenate([q_ref[0, :, hd * WA_DH:(hd + 1) * WA_DH] for hd in heads], axis=0)
            q = (q * (WA_DH ** -0.5)).astype(BF)
            sink = jnp.concatenate([jnp.broadcast_to(sink_ref[:, hd:hd + 1], (1, TQ_WA)) for hd in heads], axis=1)
            sc = _nt_dot(kc, q)
            mx = jnp.maximum(jnp.max(sc, axis=0, keepdims=True), sink)
            if local:
                kl = k_ref[0, pl.ds(start, span), cs].astype(BF)
                vl = v_ref[0, pl.ds(start, span), cs].astype(BF)
                sl = jnp.where(mask, _nt_dot(kl, q), NEG_INF)
                mx = jnp.maximum(mx, jnp.max(sl, axis=0, keepdims=True))
            ec = jnp.exp(sc - mx)
            den = jnp.sum(ec, axis=0, keepdims=True) + jnp.exp(sink - mx)
            num = _tn_dot(vc, ec.astype(BF))
            if local:
                el = jnp.exp(sl - mx)
                den = den + jnp.sum(el, axis=0, keepdims=True)
                num = num + _tn_dot(vl, el.astype(BF))
            o = (num / den).T
            out_ref[0, :, g * WA_GROUP * WA_DH:(g + 1) * WA_GROUP * WA_DH] = jnp.concatenate(
                [o[hh * TQ_WA:(hh + 1) * TQ_WA] for hh in range(WA_GROUP)], axis=1)

    @pl.when(i < n_ctx_blk)
    def _():
        attend(False)

    @pl.when(i >= n_ctx_blk)
    def _():
        attend(True)


def _window_attention(y, sink, *, col_q, col_k, col_v):
    bsz, t, _ = y.shape
    qw = WA_HEADS * WA_DH
    kw = WA_KV_HEADS * WA_DH
    return pl.pallas_call(
        functools.partial(_wa_kernel, t=t),
        grid=(bsz, t // TQ_WA),
        in_specs=[pl.BlockSpec((1, TQ_WA, qw), lambda b, i: (b, i, col_q // qw)),
                  pl.BlockSpec((1, t, kw), lambda b, i: (b, 0, col_k // kw)),
                  pl.BlockSpec((1, t, kw), lambda b, i: (b, 0, col_v // kw)),
                  pl.BlockSpec((1, WA_HEADS), lambda b, i: (0, 0))],
        out_specs=pl.BlockSpec((1, TQ_WA, qw), lambda b, i: (b, i, 0)),
        out_shape=jax.ShapeDtypeStruct((bsz, t, qw), F32),
        compiler_params=_cparams(("parallel", "parallel")),
    )(y, y, y, sink.reshape(1, WA_HEADS))


def _pop_max(x, rowid):
    m = jnp.max(x, axis=0, keepdims=True)
    first = jnp.min(jnp.where(x == m, rowid, float(x.shape[0])), axis=0, keepdims=True)
    hit = rowid == first
    return m, first, hit, jnp.where(hit, NEG_INF, x)


def _top_values(x, k):
    half = x.shape[0] // 2
    top, bot = x[:half], x[half:]
    idx = lax.broadcasted_iota(jnp.int32, top.shape, 0).astype(F32)
    swap = bot > top
    hi_v, lo_v = jnp.maximum(top, bot), jnp.minimum(top, bot)
    hi_i, lo_i = jnp.where(swap, idx + half, idx), jnp.where(swap, idx, idx + half)
    for _ in range(k):
        m = jnp.max(hi_v, axis=0, keepdims=True)
        first = jnp.min(jnp.where(hi_v == m, hi_i, float(2 * half)), axis=0, keepdims=True)
        hit = hi_i == first
        hi_v, hi_i = jnp.where(hit, lo_v, hi_v), jnp.where(hit, lo_i, hi_i)
        lo_v = jnp.where(hit, NEG_INF, lo_v)
        yield m, first


_CAND = [(a, b) for a in range(PEER_TOPK) for b in range(PEER_TOPK) if (a + 1) * (b + 1) <= PEER_TOPK]


def _peer_kernel(q_ref, xn_ref, keys_ref, ut_ref, v_ref, h_ref, mod_ref, out_ref,
                 i1_s, i2_s, gt_s, i1t_s, i2t_s, gtt_s, gsa_s, gsb_s, w_s, acc_s, *, gate_idx):
    r = pl.program_id(0)
    e = pl.program_id(1)
    tn = h_ref.shape[0]
    nslot = PEER_HEADS * PEER_TOPK

    @pl.when((e == 0) & (r == 0))
    def _init():
        i1_s[...] = jnp.zeros_like(i1_s)
        i2_s[...] = jnp.zeros_like(i2_s)
        gt_s[...] = jnp.zeros_like(gt_s)
        gsa_s[...] = jnp.zeros_like(gsa_s)
        gsb_s[...] = jnp.zeros_like(gsb_s)
        w_s[...] = jnp.zeros_like(w_s)
        acc_s[...] = jnp.zeros_like(acc_s)

    @pl.when(e == 0)
    def _slots():
        i1t_s[...] = i1_s[...].T
        i2t_s[...] = i2_s[...].T
        gtt_s[...] = gt_s[...].T

    def route(tok):
        nt = tok.stop - tok.start
        crow = lax.broadcasted_iota(jnp.int32, (len(_CAND), nt), 0).astype(F32)
        tops, idxs = [], []
        for p in range(2):
            vals, ids = [], []
            for m, first in _top_values(_nt_dot(keys_ref[0, p], q_ref[p, tok, :]), PEER_TOPK):
                vals.append(m)
                ids.append(first)
                yield
            tops.append(vals)
            idxs.append(ids)
        work = jnp.concatenate([tops[0][a] + tops[1][b] for a, b in _CAND], axis=0)
        ci1 = jnp.concatenate([idxs[0][a] for a, _ in _CAND], axis=0)
        ci2 = jnp.concatenate([idxs[1][b] for _, b in _CAND], axis=0)
        sv, s1, s2 = [], [], []
        for _ in range(PEER_TOPK):
            m, _, hit, work = _pop_max(work, crow)
            sv.append(m)
            s1.append(jnp.max(jnp.where(hit, ci1, -1.0), axis=0, keepdims=True))
            s2.append(jnp.max(jnp.where(hit, ci2, -1.0), axis=0, keepdims=True))
            yield
        ev = jnp.exp(jnp.concatenate(sv, axis=0) - sv[0])
        rows = pl.ds(pl.multiple_of(e * PEER_TOPK, PEER_TOPK), PEER_TOPK)
        i1_s[rows, tok] = jnp.concatenate(s1, axis=0)
        i2_s[rows, tok] = jnp.concatenate(s2, axis=0)
        gt_s[rows, tok] = ev * (0.5 / jnp.sum(ev, axis=0, keepdims=True))

    def build_gates(gs_write):
        sub = lax.broadcasted_iota(jnp.int32, (PEER_NKEYS, nslot), 0).astype(F32)
        per_step = tn // PEER_HEADS
        for j in range(per_step):
            n = e * per_step + j
            at = jnp.where(sub == i1t_s[pl.ds(n, 1), :], gtt_s[pl.ds(n, 1), :], 0.0).astype(BF)
            bt = jnp.where(sub == i2t_s[pl.ds(n, 1), :], 1.0, 0.0).astype(BF)
            g = pltpu.bitcast(_nt_dot(at, bt).astype(BF), jnp.uint32)
            gs_write[pl.ds(pl.multiple_of(n * GS_PITCH, 8), _words(PEER_NKEYS)), :] = g
            yield

    def step(gs_read, gs_write):
        routers = [route(slice(k * 128, (k + 1) * 128)) for k in range(tn // 128)]
        gater = build_gates(gs_write)
        n_chunk = PEER_TE // PEER_CW
        pops_per_chunk = 3 * PEER_TOPK // n_chunk
        gates_per_chunk = tn // PEER_HEADS // n_chunk
        per_chunk = PEER_CW // PEER_NKEYS
        chunks_per_out = n_chunk * PEER_CW // D
        cur = e % 2
        xn = pltpu.bitcast(xn_ref[...], BF)
        for router in routers:
            next(router)
        for c in range(n_chunk):
            if c % chunks_per_out == 0:
                cols = slice(c // chunks_per_out * PEER_CW, (c // chunks_per_out + 1) * PEER_CW)
                acc_s[:, cols] += _dot(w_s[1 - cur], pltpu.bitcast(v_ref[:, cols], BF))
            a = _dot(xn, pltpu.bitcast(ut_ref[:, c * PEER_CW:(c + 1) * PEER_CW], BF))
            for _ in range(pops_per_chunk):
                for router in routers:
                    next(router, None)
            for _ in range(gates_per_chunk):
                next(gater, None)
            ws = []
            for k in range(0, per_chunk, 2):
                pair = (e * (PEER_TE // PEER_NKEYS) + c * per_chunk + k) // 2
                words = gs_read[pl.ds(pair, tn, stride=GS_PITCH), :]
                for half, g in enumerate((pltpu.bitcast(words << 16, F32),
                                          pltpu.bitcast(words & jnp.uint32(0xFFFF0000), F32))):
                    ak = a[:, (k + half) * PEER_NKEYS:(k + half + 1) * PEER_NKEYS]
                    ws.append((g * (ak * (1.0 + lax.erf(ak * (0.5 ** 0.5))))).astype(BF))
            w_s[cur, :, c * PEER_CW:(c + 1) * PEER_CW] = jnp.concatenate(ws, axis=1)
        for gen in routers + [gater]:
            for _ in gen:
                pass

    @pl.when(r % 2 == 0)
    def _even():
        step(gsa_s, gsb_s)

    @pl.when(r % 2 == 1)
    def _odd():
        step(gsb_s, gsa_s)

    @pl.when(e == 0)
    def _fin():
        gate = mod_ref[0, 0][gate_idx:gate_idx + 1]
        out_ref[...] = h_ref[...] + gate * acc_s[...]
        acc_s[...] = jnp.zeros_like(acc_s)


def _peer_residual(h2, q2, xn2, keys, ut, v, layer, modtab, *, gate_idx, tok_per_batch, ctx_rows):
    ntok = h2.shape[0]
    nexp = ut.shape[2]
    nslot = PEER_HEADS * PEER_TOPK
    tn, te = PEER_TN, PEER_TE
    ntiles = ntok // tn
    tiles_per_batch = tok_per_batch // tn
    assert nexp // te == PEER_HEADS

    nsteps = nexp // te

    def lagged(r, lag):
        return jnp.clip(r - lag, 0, ntiles - 1)

    def mod_map(r, e):
        t = lagged(r, 3)
        return (t // tiles_per_batch, _is_latent_tile(t % tiles_per_batch, ctx_rows // tn), 0, 0)

    return pl.pallas_call(
        functools.partial(_peer_kernel, gate_idx=gate_idx),
        grid=(ntiles + 3, nsteps),
        in_specs=[pl.BlockSpec((2, tn, PEER_DHALF), lambda r, e: (e, lagged(r, 0), 0)),
                  pl.BlockSpec((_words(tn), D), lambda r, e: (lagged(r, 2), 0)),
                  pl.BlockSpec((1, 2, PEER_NKEYS, PEER_DHALF), lambda r, e: (e, 0, 0, 0)),
                  pl.BlockSpec((None, _words(D), te), lambda r, e: (layer, 0, e)),
                  pl.BlockSpec((None, _words(te), D), lambda r, e: (layer, (e + nsteps - 1) % nsteps, 0)),
                  pl.BlockSpec((tn, D), lambda r, e: (lagged(r, 3), 0)),
                  pl.BlockSpec((1, 1, N_MOD, D), mod_map)],
        out_specs=pl.BlockSpec((tn, D), lambda r, e: (lagged(r, 3), 0)),
        out_shape=jax.ShapeDtypeStruct((ntok, D), F32),
        scratch_shapes=[pltpu.VMEM((nslot, tn), F32),
                        pltpu.VMEM((nslot, tn), F32),
                        pltpu.VMEM((nslot, tn), F32),
                        pltpu.VMEM((tn, nslot), F32),
                        pltpu.VMEM((tn, nslot), F32),
                        pltpu.VMEM((tn, nslot), F32),
                        pltpu.VMEM((tn * GS_PITCH, PEER_NKEYS), jnp.uint32),
                        pltpu.VMEM((tn * GS_PITCH, PEER_NKEYS), jnp.uint32),
                        pltpu.VMEM((2, tn, te), BF),
                        pltpu.VMEM((tn, D), F32)],
        compiler_params=_cparams(("arbitrary", "arbitrary")),
    )(q2, xn2, keys, ut, v, h2, modtab)


def _pack_kernel(w_ref, o_ref, *, transpose):
    w = w_ref[0].T if transpose else w_ref[0]
    o_ref[0] = pltpu.bitcast(w.astype(BF), jnp.uint32)


def _pack_experts(w, *, transpose):
    nl, ne, d = w.shape
    te = 512
    if transpose:
        out_block, out_map, out_dims = (1, _words(d), te), (lambda l, i: (l, 0, i)), (nl, _words(d), ne)
    else:
        out_block, out_map, out_dims = (1, _words(te), d), (lambda l, i: (l, i, 0)), (nl, _words(ne), d)
    return pl.pallas_call(
        functools.partial(_pack_kernel, transpose=transpose),
        grid=(nl, ne // te),
        in_specs=[pl.BlockSpec((1, te, d), lambda l, i: (l, i, 0))],
        out_specs=pl.BlockSpec(out_block, out_map),
        out_shape=jax.ShapeDtypeStruct(out_dims, jnp.uint32),
        compiler_params=_cparams(("parallel", "parallel")),
    )(w)


def _final_kernel(h_ref, g_ref, o_ref):
    x = h_ref[0]
    o_ref[0] = x * lax.rsqrt(jnp.mean(x * x, axis=-1, keepdims=True) + EPS) * g_ref[...]


def _final_norm(h, g):
    bsz, s, _ = h.shape
    return pl.pallas_call(
        _final_kernel,
        grid=(bsz, s // TM),
        in_specs=[pl.BlockSpec((1, TM, D), lambda b, i: (b, i, 0)),
                  pl.BlockSpec((1, D), lambda b, i: (0, 0))],
        out_specs=pl.BlockSpec((1, TM, D), lambda b, i: (b, i, 0)),
        out_shape=jax.ShapeDtypeStruct((bsz, s, D), F32),
        compiler_params=_cparams(("parallel", "parallel")),
    )(h, g.reshape(1, D))


def _rot_half_cols(w, dh):
    d, n = w.shape
    wh = w.reshape(d, n // dh, 2, dh // 2)
    return jnp.concatenate([-wh[:, :, 1], wh[:, :, 0]], axis=2).reshape(d, n)


def _rope_tables(s, dh, width):
    rows = s // GRID_W
    r = jnp.repeat(jnp.arange(rows, dtype=F32), GRID_W)
    col = jnp.broadcast_to(jnp.arange(GRID_W, dtype=F32), (rows, GRID_W)).reshape(-1)
    nf = dh // 4
    inv = ROPE_THETA ** (-jnp.arange(nf, dtype=F32) / nf)
    ang = jnp.concatenate([r[:, None] * inv, col[:, None] * inv], axis=-1)
    cos = jnp.concatenate([jnp.ones((CTX, dh // 2), F32), jnp.cos(ang)], axis=0)
    sin = jnp.concatenate([jnp.zeros((CTX, dh // 2), F32), jnp.sin(ang)], axis=0)
    reps = width // (dh // 2)
    return jnp.tile(cos, (1, reps)), jnp.tile(sin, (1, reps))


def _interleave_heads(a, nh, dh):
    lead = a.shape[:-1]
    return a.reshape(lead + (2, nh, dh)).swapaxes(-3, -2).reshape(lead + (2 * nh * dh,))


def kernel(x, c, ctx, c_ctx, mod_w, mod_b, norm1_g, norm2_g, ev_w_in, ev_ml_conv_w, ev_ml_conv_b,
           ev_ml_gate_b, ev_ml_norm_g, ev_da_lam, ev_da_norm_g, ev_w_out, od_w_in, od_sink, od_w_out,
           pr_w_q, pr_keys, pr_u, pr_v, final_g):
    bsz, s, _ = x.shape
    depth = mod_w.shape[0]
    t = CTX + s
    ml_w = ML_HEADS * ML_DV
    da_w = DA_HEADS * DA_DV
    qk_w = 2 * ML_HEADS * ML_DQK
    da_qw = DA_HEADS * 2 * DA_DQK

    h = jnp.concatenate([ctx, x], axis=1)

    rows = -(-(bsz + 1) // 8) * 8
    cc = jnp.zeros((rows, D), F32).at[:bsz].set(c).at[bsz].set(c_ctx)
    mods = _modulation(cc, mod_w, mod_b)
    mod_lat = mods[:, :bsz].reshape(depth, bsz, N_MOD, D)
    mod_ctx = jnp.broadcast_to(mods[:, bsz].reshape(depth, 1, N_MOD, D), (depth, bsz, N_MOD, D))
    modtabs = jnp.stack([mod_ctx, mod_lat], axis=2)

    wa_rope_w = (WA_HEADS + WA_KV_HEADS) * WA_DH
    cos_t, sin_t = _rope_tables(s, WA_DH, wa_rope_w)
    ut_words = _pack_experts(pr_u, transpose=True)
    v_words = _pack_experts(pr_v, transpose=False)

    for i in range(depth):
        j = i // 2
        modtab = modtabs[i]
        if i % 2 == 0:
            lam_init = 0.8 - 0.6 * math.exp(-0.3 * i)
            w = ev_w_in[j]
            o0 = 0
            w_mqk = w[:, o0:o0 + qk_w]; o0 += qk_w
            w_mv = w[:, o0:o0 + ml_w]; o0 += ml_w
            w_mo = w[:, o0:o0 + ml_w]; o0 += ml_w
            w_g = w[:, o0:o0 + 4 * ML_HEADS]; o0 += 4 * ML_HEADS
            w_dq = w[:, o0:o0 + da_qw]; o0 += da_qw
            w_dk = w[:, o0:o0 + da_qw]; o0 += da_qw
            w_dv = w[:, o0:o0 + da_w]
            w_rope = jnp.concatenate([w_dq, w_dk], axis=1)
            gpad = jnp.zeros((D, 128 - 4 * ML_HEADS), F32)
            wm = jnp.concatenate([w_rope, _interleave_heads(w_mqk, ML_HEADS, ML_DQK), w_mv, w_mo, w_dv, w_g, gpad],
                                 axis=1).astype(BF)
            wr = _rot_half_cols(w_rope, DA_DQK).astype(BF)
            nr = 2 * da_qw
            col_mqk = nr
            col_mv = col_mqk + qk_w
            col_mo = col_mv + ml_w
            col_dv = col_mo + ml_w
            col_g = col_dv + da_w
            (y,) = _norm_mod_proj(h, modtab, norm1_g[i], wm, wr, cos_t[:, :nr], sin_t[:, :nr],
                                  shift_idx=0, scale_idx=1)
            gates_t = jnp.swapaxes(y[:, :, col_g:col_g + 4 * ML_HEADS], 1, 2)
            gk, qs = _gate_scan(gates_t, ev_ml_gate_b[j])
            qs = qs.reshape(bsz, 4, ML_HEADS, t).transpose(0, 2, 3, 1)
            ml = _mlstm(y, gk, qs, _interleave_heads(ev_ml_conv_w[j], ML_HEADS, ML_DQK),
                        _interleave_heads(ev_ml_conv_b[j], ML_HEADS, ML_DQK), ev_ml_norm_g[j],
                        col_qk=col_mqk, col_v=col_mv, col_o=col_mo)
            da = _diff_attention(y, ev_da_lam[j], ev_da_norm_g[j], col_q=0, col_k=da_qw, col_v=col_dv,
                                 lam_init=lam_init)
            w_out = ev_w_out[j].astype(BF)
            h = _outproj_residual(h, modtab, [(ml, w_out[:ml_w]), (da, w_out[ml_w:])], gate_idx=2)
        else:
            w = od_w_in[j]
            wm = w.astype(BF)
            wr = _rot_half_cols(w[:, :wa_rope_w], WA_DH).astype(BF)
            (y,) = _norm_mod_proj(h, modtab, norm1_g[i], wm, wr, cos_t, sin_t, shift_idx=0, scale_idx=1)
            att = _window_attention(y, od_sink[j], col_q=0, col_k=WA_HEADS * WA_DH,
                                    col_v=WA_HEADS * WA_DH + WA_KV_HEADS * WA_DH)
            h = _outproj_residual(h, modtab, [(att, od_w_out[j].astype(BF))], gate_idx=2)
        ctx_rows = CTX if i < depth - 1 else 0
        hp = h if ctx_rows else h[:, CTX:]
        tp = hp.shape[1]
        q, xn = _norm_mod_proj(hp, modtab, norm2_g[i], pr_w_q[i].astype(BF), shift_idx=3, scale_idx=4,
                               want_xn=True, split=PEER_DHALF, ctx_rows=ctx_rows, tm=_row_tile(tp))
        h = _peer_residual(hp.reshape(bsz * tp, D), q, xn.reshape(-1, D), pr_keys[i].astype(BF),
                           ut_words, v_words, i, modtab,
                           gate_idx=5, tok_per_batch=tp, ctx_rows=ctx_rows).reshape(bsz, tp, D)
    return _final_norm(h, final_g)
```

```python
import functools
import math

import jax
import jax.numpy as jnp
from jax import lax
from jax.experimental import pallas as pl
from jax.experimental.pallas import tpu as pltpu

D = 1024
CTX = 256
GRID_W = 64
EPS = 1e-6
ROPE_THETA = 10000.0
N_MOD = 6

ML_HEADS = 4
ML_DQK = 64
ML_DV = 128
DA_HEADS = 4
DA_DQK = 64
DA_DV = 128
WA_HEADS = 16
WA_KV_HEADS = 4
WA_GROUP = 4
WA_DH = 64
WINDOW = 128

PEER_HEADS = 8
PEER_NKEYS = 128
PEER_DHALF = 128
PEER_TOPK = 16

TM = 256
TQ_ML = 256
TQ_DA = 256
TQ_WA = 128
PEER_TN = 256
PEER_TE = 2048
PEER_CW = 512
GS_PITCH = 72
VMEM_LIMIT = 56 * 1024 * 1024

BF = jnp.bfloat16
F32 = jnp.float32
NEG_INF = float("-inf")


def _cparams(sem):
    return pltpu.CompilerParams(dimension_semantics=sem, vmem_limit_bytes=VMEM_LIMIT)


def _nt_dot(a, b):
    return lax.dot_general(a, b, (((1,), (1,)), ((), ())), preferred_element_type=F32)


def _tn_dot(a, b):
    return lax.dot_general(a, b, (((0,), (0,)), ((), ())), preferred_element_type=F32)


def _dot(a, b):
    return jnp.dot(a, b, preferred_element_type=F32)


def _words(rows):
    return rows * jnp.dtype(BF).itemsize // 4


def _mod_kernel(cc_ref, w_ref, b_ref, o_ref):
    a = jax.nn.silu(cc_ref[...]).astype(BF)
    o_ref[0] = _dot(a, w_ref[0].astype(BF)) + b_ref[0]


def _modulation(cc, mod_w, mod_b):
    depth = mod_w.shape[0]
    rows = cc.shape[0]
    tn = 1536
    return pl.pallas_call(
        _mod_kernel,
        grid=(depth, N_MOD * D // tn),
        in_specs=[pl.BlockSpec((rows, D), lambda l, n: (0, 0)),
                  pl.BlockSpec((1, D, tn), lambda l, n: (l, 0, n)),
                  pl.BlockSpec((1, 1, tn), lambda l, n: (l, 0, n))],
        out_specs=pl.BlockSpec((1, rows, tn), lambda l, n: (l, 0, n)),
        out_shape=jax.ShapeDtypeStruct((depth, rows, N_MOD * D), F32),
        compiler_params=_cparams(("parallel", "parallel")),
    )(cc, mod_w, mod_b.reshape(depth, 1, N_MOD * D))


def _is_latent_tile(tile, ctx_tiles):
    return jnp.minimum(tile // ctx_tiles, 1) if ctx_tiles else 1


def _mod_spec():
    return pl.BlockSpec((1, 2, N_MOD, D), lambda b, t: (b, 0, 0, 0))


def _mod_rows(mod_ref, idx, tm, ctx_rows):
    lat = mod_ref[0, 1][idx:idx + 1]
    if not ctx_rows:
        return lat
    row = pl.program_id(1) * tm + lax.broadcasted_iota(jnp.int32, (tm, 1), 0)
    return jnp.where(row < ctx_rows, mod_ref[0, 0][idx:idx + 1], lat)


def _row_tile(t):
    return next(tm for tm in (768, 512, TM) if t % tm == 0)


def _proj_kernel(*refs, shift_idx, scale_idx, nrope, want_xn, split, ctx_rows):
    it = iter(refs)
    h_ref, mod_ref, g_ref, wm_ref = next(it), next(it), next(it), next(it)
    if nrope:
        wr_ref, cos_ref, sin_ref = next(it), next(it), next(it)
    y_ref = next(it)
    x = h_ref[0]
    tm = x.shape[0]
    xn = x * lax.rsqrt(jnp.mean(x * x, axis=-1, keepdims=True) + EPS) * g_ref[...]
    xm = xn * (1.0 + _mod_rows(mod_ref, scale_idx, tm, ctx_rows)) + _mod_rows(mod_ref, shift_idx, tm, ctx_rows)
    xb = xm.astype(BF)
    acc = _dot(xb, wm_ref[...])
    if nrope:
        rot = _dot(xb, wr_ref[...])
        y_ref[0, :, :nrope] = acc[:, :nrope] * cos_ref[...] + rot * sin_ref[...]
        y_ref[0, :, nrope:] = acc[:, nrope:]
    elif split:
        for c in range(acc.shape[1] // split):
            y_ref[c] = acc[:, c * split:(c + 1) * split].astype(y_ref.dtype)
    else:
        y_ref[0] = acc
    if want_xn:
        next(it)[0] = pltpu.bitcast(xb, jnp.uint32)


def _norm_mod_proj(h, modtab, g, wm, wr=None, cos=None, sin=None, *, shift_idx, scale_idx, want_xn=False,
                   split=0, ctx_rows=CTX, tm=TM):
    bsz, t, _ = h.shape
    nm = wm.shape[1]
    nrope = 0 if wr is None else wr.shape[1]
    in_specs = [pl.BlockSpec((1, tm, D), lambda b, i: (b, i, 0)),
                _mod_spec(),
                pl.BlockSpec((1, D), lambda b, i: (0, 0)),
                pl.BlockSpec((D, nm), lambda b, i: (0, 0))]
    args = [h, modtab, g.reshape(1, D), wm]
    if nrope:
        in_specs += [pl.BlockSpec((D, nrope), lambda b, i: (0, 0)),
                     pl.BlockSpec((tm, nrope), lambda b, i: (i, 0)),
                     pl.BlockSpec((tm, nrope), lambda b, i: (i, 0))]
        args += [wr, cos, sin]
    out_specs = [pl.BlockSpec((1, tm, nm), lambda b, i: (b, i, 0))]
    out_shape = [jax.ShapeDtypeStruct((bsz, t, nm), F32)]
    if split:
        out_specs = [pl.BlockSpec((nm // split, tm, split), lambda b, i: (0, b * (t // tm) + i, 0))]
        out_shape = [jax.ShapeDtypeStruct((nm // split, bsz * t, split), BF)]
    if want_xn:
        out_specs.append(pl.BlockSpec((1, _words(tm), D), lambda b, i: (b, i, 0)))
        out_shape.append(jax.ShapeDtypeStruct((bsz, _words(t), D), jnp.uint32))
    return pl.pallas_call(
        functools.partial(_proj_kernel, shift_idx=shift_idx, scale_idx=scale_idx, nrope=nrope, want_xn=want_xn,
                          split=split, ctx_rows=ctx_rows),
        grid=(bsz, t // tm), in_specs=in_specs, out_specs=out_specs, out_shape=out_shape,
        compiler_params=_cparams(("parallel", "parallel")),
    )(*args)


def _outproj_kernel(*refs, gate_idx, n_src):
    h_ref, mod_ref = refs[0], refs[1]
    o_ref = refs[2 + 2 * n_src]
    acc = None
    for s in range(n_src):
        part = _dot(refs[2 + 2 * s][0].astype(BF), refs[3 + 2 * s][...])
        acc = part if acc is None else acc + part
    o_ref[0] = h_ref[0] + _mod_rows(mod_ref, gate_idx, acc.shape[0], CTX) * acc


def _outproj_residual(h, modtab, srcs, *, gate_idx):
    bsz, t, _ = h.shape
    tm = _row_tile(t)
    in_specs = [pl.BlockSpec((1, tm, D), lambda b, i: (b, i, 0)), _mod_spec()]
    args = [h, modtab]
    for x, w in srcs:
        k = x.shape[-1]
        in_specs += [pl.BlockSpec((1, tm, k), lambda b, i: (b, i, 0)),
                     pl.BlockSpec((k, D), lambda b, i: (0, 0))]
        args += [x, w]
    return pl.pallas_call(
        functools.partial(_outproj_kernel, gate_idx=gate_idx, n_src=len(srcs)),
        grid=(bsz, t // tm), in_specs=in_specs,
        out_specs=pl.BlockSpec((1, tm, D), lambda b, i: (b, i, 0)),
        out_shape=jax.ShapeDtypeStruct((bsz, t, D), F32),
        compiler_params=_cparams(("parallel", "parallel")),
    )(*args)


def _lane_scan(x, op, ident, lane, reverse):
    for k in range(7):
        s = 1 << k
        if reverse:
            sh = pltpu.roll(x, 128 - s, 1)
            x = op(x, jnp.where(lane < 128 - s, sh, ident))
        else:
            sh = pltpu.roll(x, s, 1)
            x = op(x, jnp.where(lane >= s, sh, ident))
    return x


def _gate_scan_kernel(g_ref, b_ref, gk_ref, qs_ref, *, t):
    nb = t // 128
    nh = ML_HEADS
    lane = lax.broadcasted_iota(jnp.int32, (nh, 128), 1)
    pre = g_ref[0] + b_ref[...]
    for d in range(2):
        ig = pre[(2 * d) * nh:(2 * d + 1) * nh]
        lf = jax.nn.log_sigmoid(pre[(2 * d + 1) * nh:(2 * d + 2) * nh])
        if d == 0:
            order = list(range(nb))
        else:
            order = list(range(CTX // 128 - 1, -1, -1)) + list(range(nb - 1, CTX // 128 - 1, -1))
        carry_b = jnp.zeros((nh, 1), F32)
        carry_m = jnp.full((nh, 1), NEG_INF, F32)
        edge = 0 if d else 127
        for blk in order:
            sl = slice(blk * 128, (blk + 1) * 128)
            bb = _lane_scan(lf[:, sl], jnp.add, 0.0, lane, bool(d)) + carry_b
            gg = ig[:, sl] - bb
            mm = jnp.maximum(_lane_scan(gg, jnp.maximum, NEG_INF, lane, bool(d)), carry_m)
            carry_b = bb[:, edge:edge + 1]
            carry_m = mm[:, edge:edge + 1]
            gk_ref[0, d * nh:(d + 1) * nh, sl] = gg
            qs_ref[0, (2 * d) * nh:(2 * d + 1) * nh, sl] = mm
            qs_ref[0, (2 * d + 1) * nh:(2 * d + 2) * nh, sl] = bb + mm


def _gate_scan(gates_t, gate_b):
    bsz, ng, t = gates_t.shape
    return pl.pallas_call(
        functools.partial(_gate_scan_kernel, t=t),
        grid=(bsz,),
        in_specs=[pl.BlockSpec((1, ng, t), lambda b: (b, 0, 0)),
                  pl.BlockSpec((ng, 1), lambda b: (0, 0))],
        out_specs=[pl.BlockSpec((1, 2 * ML_HEADS, t), lambda b: (b, 0, 0)),
                   pl.BlockSpec((1, 4 * ML_HEADS, t), lambda b: (b, 0, 0))],
        out_shape=[jax.ShapeDtypeStruct((bsz, 2 * ML_HEADS, t), F32),
                   jax.ShapeDtypeStruct((bsz, 4 * ML_HEADS, t), F32)],
        compiler_params=_cparams(("parallel",)),
    )(gates_t, gate_b.reshape(ng, 1))


def _mlstm_kernel(qk_ref, v_ref, o_ref, gk_ref, qs_ref, cw_ref, cb_ref, ng_ref, out_ref,
                  q_s, k_s, v_s, *, t):
    hd = pl.program_id(1)
    x = qk_ref[0]
    row = lax.broadcasted_iota(jnp.int32, (t, 1), 0)
    xp = jnp.where((row == 0) | (row == CTX), 0.0, pltpu.roll(x, 1, 0))
    xn = jnp.where((row == CTX - 1) | (row == t - 1), 0.0, pltpu.roll(x, t - 1, 0))
    cw = cw_ref[...]
    y = cb_ref[...] + xp * cw[0:1] + x * cw[1:2] + xn * cw[2:3]
    y = jax.nn.silu(y)
    q_s[...] = y[:, :ML_DQK].astype(BF)
    k_s[...] = (y[:, ML_DQK:] * (ML_DQK ** -0.5)).astype(BF)
    v_s[...] = v_ref[0].astype(BF)
    lower = lax.broadcasted_iota(jnp.int32, (1, TQ_ML), 1) <= lax.broadcasted_iota(jnp.int32, (TQ_ML, 1), 0)
    upper = lax.broadcasted_iota(jnp.int32, (1, TQ_ML), 1) >= lax.broadcasted_iota(jnp.int32, (TQ_ML, 1), 0)

    for qi in range(t // TQ_ML):
        r0 = qi * TQ_ML
        rows = slice(r0, r0 + TQ_ML)
        s = _nt_dot(q_s[rows, :], k_s[...])
        qs = qs_ref[0, 0, rows, :]
        fwd = ([(0, r0, None)] if r0 else []) + [(r0, r0 + TQ_ML, lower)]
        seg_end = CTX if r0 < CTX else t
        bwd = ([(0, CTX, None)] if r0 >= CTX else []) + [(r0, r0 + TQ_ML, upper)]
        bwd += [(r0 + TQ_ML, seg_end, None)] if r0 + TQ_ML < seg_end else []
        hsum = None
        for d, pieces in enumerate((fwd, bwd)):
            gk = gk_ref[0, pl.ds(d * ML_HEADS + hd, 1), :]
            den = num = None
            for a, b, tri in pieces:
                z = gk[:, a:b] - qs[:, 2 * d:2 * d + 1]
                p = s[:, a:b] * jnp.exp(z if tri is None else jnp.where(tri, z, NEG_INF))
                dp = jnp.sum(p, axis=-1, keepdims=True)
                np_ = _dot(p.astype(BF), v_s[a:b, :])
                den = dp if den is None else den + dp
                num = np_ if num is None else num + np_
            hdir = num / jnp.maximum(jnp.abs(den), jnp.exp(-qs[:, 2 * d + 1:2 * d + 2]))
            hsum = hdir if hsum is None else hsum + hdir
        yn = hsum * lax.rsqrt(jnp.mean(hsum * hsum, axis=-1, keepdims=True) + EPS) * ng_ref[...]
        out_ref[0, rows, :] = yn * jax.nn.sigmoid(o_ref[0, rows, :])


def _mlstm(y, gk, qs, conv_w, conv_b, norm_g, *, col_qk, col_v, col_o):
    bsz, t, _ = y.shape
    w = ML_HEADS * ML_DV
    return pl.pallas_call(
        functools.partial(_mlstm_kernel, t=t),
        grid=(bsz, ML_HEADS),
        in_specs=[pl.BlockSpec((1, t, 128), lambda b, h: (b, 0, col_qk // 128 + h)),
                  pl.BlockSpec((1, t, 128), lambda b, h: (b, 0, col_v // 128 + h)),
                  pl.BlockSpec((1, t, 128), lambda b, h: (b, 0, col_o // 128 + h)),
                  pl.BlockSpec((1, 2 * ML_HEADS, t), lambda b, h: (b, 0, 0)),
                  pl.BlockSpec((1, 1, t, 4), lambda b, h: (b, h, 0, 0)),
                  pl.BlockSpec((3, 128), lambda b, h: (0, h)),
                  pl.BlockSpec((1, 128), lambda b, h: (0, h)),
                  pl.BlockSpec((1, 128), lambda b, h: (0, h))],
        out_specs=pl.BlockSpec((1, t, 128), lambda b, h: (b, 0, h)),
        out_shape=jax.ShapeDtypeStruct((bsz, t, w), F32),
        scratch_shapes=[pltpu.VMEM((t, ML_DQK), BF), pltpu.VMEM((t, ML_DQK), BF), pltpu.VMEM((t, ML_DV), BF)],
        compiler_params=_cparams(("parallel", "parallel")),
    )(y, y, y, gk, qs, conv_w, conv_b.reshape(1, w), norm_g.reshape(1, w))


def _da_kernel(q_ref, k_ref, v_ref, lam_ref, ng_ref, out_ref, *, t, lam_init):
    qt = pl.program_id(2)
    lp = lam_ref[...]
    lam = (jnp.exp(jnp.sum(lp[0:1] * lp[1:2], axis=-1, keepdims=True))
           - jnp.exp(jnp.sum(lp[2:3] * lp[3:4], axis=-1, keepdims=True)) + lam_init)
    q = q_ref[0] * (DA_DQK ** -0.5)

    def attend(nk):
        v = v_ref[0, :nk, :].astype(BF)
        outs = []
        for m in range(2):
            qm = q[:, m * DA_DQK:(m + 1) * DA_DQK].astype(BF)
            km = k_ref[0, :nk, m * DA_DQK:(m + 1) * DA_DQK].astype(BF)
            s = _nt_dot(qm, km)
            e = jnp.exp(s - jnp.max(s, axis=-1, keepdims=True))
            outs.append(_dot(e.astype(BF), v) / jnp.sum(e, axis=-1, keepdims=True))
        o = outs[0] - lam * outs[1]
        yn = o * lax.rsqrt(jnp.mean(o * o, axis=-1, keepdims=True) + EPS) * ng_ref[...]
        out_ref[0] = yn * (1.0 - lam_init)

    @pl.when(qt < CTX // TQ_DA)
    def _():
        attend(CTX)

    @pl.when(qt >= CTX // TQ_DA)
    def _():
        attend(t)


def _diff_attention(y, da_lam, norm_g, *, col_q, col_k, col_v, lam_init):
    bsz, t, _ = y.shape
    w = DA_HEADS * DA_DV
    return pl.pallas_call(
        functools.partial(_da_kernel, t=t, lam_init=lam_init),
        grid=(bsz, DA_HEADS, t // TQ_DA),
        in_specs=[pl.BlockSpec((1, TQ_DA, 128), lambda b, h, i: (b, i, col_q // 128 + h)),
                  pl.BlockSpec((1, t, 128), lambda b, h, i: (b, 0, col_k // 128 + h)),
                  pl.BlockSpec((1, t, 128), lambda b, h, i: (b, 0, col_v // 128 + h)),
                  pl.BlockSpec((4, DA_DQK), lambda b, h, i: (0, 0)),
                  pl.BlockSpec((1, 128), lambda b, h, i: (0, h))],
        out_specs=pl.BlockSpec((1, TQ_DA, 128), lambda b, h, i: (b, i, h)),
        out_shape=jax.ShapeDtypeStruct((bsz, t, w), F32),
        compiler_params=_cparams(("parallel", "parallel", "parallel")),
    )(y, y, y, da_lam, norm_g.reshape(1, w))


def _wa_kernel(q_ref, k_ref, v_ref, sink_ref, out_ref, *, t):
    i = pl.program_id(1)
    n_ctx_blk = CTX // TQ_WA
    n_lat = t - CTX
    span = 3 * WINDOW

    def attend(local):
        if local:
            lat0 = jnp.clip((i - n_ctx_blk - 1) * TQ_WA, 0, n_lat - span)
            start = pl.multiple_of(CTX + lat0, TQ_WA)
            qcol = lax.broadcasted_iota(jnp.int32, (1, WA_GROUP, TQ_WA), 2).reshape(1, WA_GROUP * TQ_WA)
            kpos = lat0 + lax.broadcasted_iota(jnp.int32, (span, 1), 0)
            mask = jnp.abs((i - n_ctx_blk) * TQ_WA + qcol - kpos) <= WINDOW
        for g in range(WA_KV_HEADS):
            cs = slice(g * WA_DH, (g + 1) * WA_DH)
            heads = range(g * WA_GROUP, (g + 1) * WA_GROUP)
            kc = k_ref[0, :CTX, cs].astype(BF)
            vc = v_ref[0, :CTX, cs].astype(BF)
            q = jnp.concatenate([q_ref[0, :, hd * WA_DH:(hd + 1) * WA_DH] for hd in heads], axis=0)
            q = (q * (WA_DH ** -0.5)).astype(BF)
            sink = jnp.concatenate([jnp.broadcast_to(sink_ref[:, hd:hd + 1], (1, TQ_WA)) for hd in heads], axis=1)
            sc = _nt_dot(kc, q)
            mx = jnp.maximum(jnp.max(sc, axis=0, keepdims=True), sink)
            if local:
                kl = k_ref[0, pl.ds(start, span), cs].astype(BF)
                vl = v_ref[0, pl.ds(start, span), cs].astype(BF)
                sl = jnp.where(mask, _nt_dot(kl, q), NEG_INF)
                mx = jnp.maximum(mx, jnp.max(sl, axis=0, keepdims=True))
            ec = jnp.exp(sc - mx)
            den = jnp.sum(ec, axis=0, keepdims=True) + jnp.exp(sink - mx)
            num = _tn_dot(vc, ec.astype(BF))
            if local:
                el = jnp.exp(sl - mx)
                den = den + jnp.sum(el, axis=0, keepdims=True)
                num = num + _tn_dot(vl, el.astype(BF))
            o = (num / den).T
            out_ref[0, :, g * WA_GROUP * WA_DH:(g + 1) * WA_GROUP * WA_DH] = jnp.concatenate(
                [o[hh * TQ_WA:(hh + 1) * TQ_WA] for hh in range(WA_GROUP)], axis=1)

    @pl.when(i < n_ctx_blk)
    def _():
        attend(False)

    @pl.when(i >= n_ctx_blk)
    def _():
        attend(True)


def _window_attention(y, sink, *, col_q, col_k, col_v):
    bsz, t, _ = y.shape
    qw = WA_HEADS * WA_DH
    kw = WA_KV_HEADS * WA_DH
    return pl.pallas_call(
        functools.partial(_wa_kernel, t=t),
        grid=(bsz, t // TQ_WA),
        in_specs=[pl.BlockSpec((1, TQ_WA, qw), lambda b, i: (b, i, col_q // qw)),
                  pl.BlockSpec((1, t, kw), lambda b, i: (b, 0, col_k // kw)),
                  pl.BlockSpec((1, t, kw), lambda b, i: (b, 0, col_v // kw)),
                  pl.BlockSpec((1, WA_HEADS), lambda b, i: (0, 0))],
        out_specs=pl.BlockSpec((1, TQ_WA, qw), lambda b, i: (b, i, 0)),
        out_shape=jax.ShapeDtypeStruct((bsz, t, qw), F32),
        compiler_params=_cparams(("parallel", "parallel")),
    )(y, y, y, sink.reshape(1, WA_HEADS))


def _pop_max(x, rowid):
    m = jnp.max(x, axis=0, keepdims=True)
    first = jnp.min(jnp.where(x == m, rowid, float(x.shape[0])), axis=0, keepdims=True)
    hit = rowid == first
    return m, first, hit, jnp.where(hit, NEG_INF, x)


def _top_values(x, k):
    half = x.shape[0] // 2
    top, bot = x[:half], x[half:]
    idx = lax.broadcasted_iota(jnp.int32, top.shape, 0).astype(F32)
    swap = bot > top
    hi_v, lo_v = jnp.maximum(top, bot), jnp.minimum(top, bot)
    hi_i, lo_i = jnp.where(swap, idx + half, idx), jnp.where(swap, idx, idx + half)
    for _ in range(k):
        m = jnp.max(hi_v, axis=0, keepdims=True)
        first = jnp.min(jnp.where(hi_v == m, hi_i, float(2 * half)), axis=0, keepdims=True)
        hit = hi_i == first
        hi_v, hi_i = jnp.where(hit, lo_v, hi_v), jnp.where(hit, lo_i, hi_i)
        lo_v = jnp.where(hit, NEG_INF, lo_v)
        yield m, first


_CAND = [(a, b) for a in range(PEER_TOPK) for b in range(PEER_TOPK) if (a + 1) * (b + 1) <= PEER_TOPK]


def _peer_kernel(q_ref, xn_ref, keys_ref, ut_ref, v_ref, h_ref, mod_ref, out_ref,
                 i1_s, i2_s, gt_s, i1t_s, i2t_s, gtt_s, gsa_s, gsb_s, w_s, acc_s, *, gate_idx, ntiles):
    r = pl.program_id(0)
    e = pl.program_id(1)
    tn = h_ref.shape[0]
    nslot = PEER_HEADS * PEER_TOPK

    @pl.when((e == 0) & (r == 0))
    def _init():
        i1_s[...] = jnp.zeros_like(i1_s)
        i2_s[...] = jnp.zeros_like(i2_s)
        gt_s[...] = jnp.zeros_like(gt_s)
        gsa_s[...] = jnp.zeros_like(gsa_s)
        gsb_s[...] = jnp.zeros_like(gsb_s)
        w_s[...] = jnp.zeros_like(w_s)
        acc_s[...] = jnp.zeros_like(acc_s)

    @pl.when(e == 0)
    def _slots():
        i1t_s[...] = i1_s[...].T
        i2t_s[...] = i2_s[...].T
        gtt_s[...] = gt_s[...].T

    def route(tok):
        nt = tok.stop - tok.start
        crow = lax.broadcasted_iota(jnp.int32, (len(_CAND), nt), 0).astype(F32)
        tops, idxs = [], []
        for p in range(2):
            vals, ids = [], []
            for m, first in _top_values(_nt_dot(keys_ref[0, p], q_ref[p, tok, :]), PEER_TOPK):
                vals.append(m)
                ids.append(first)
                yield
            tops.append(vals)
            idxs.append(ids)
        work = jnp.concatenate([tops[0][a] + tops[1][b] for a, b in _CAND], axis=0)
        ci1 = jnp.concatenate([idxs[0][a] for a, _ in _CAND], axis=0)
        ci2 = jnp.concatenate([idxs[1][b] for _, b in _CAND], axis=0)
        sv, s1, s2 = [], [], []
        for _ in range(PEER_TOPK):
            m, _, hit, work = _pop_max(work, crow)
            sv.append(m)
            s1.append(jnp.max(jnp.where(hit, ci1, -1.0), axis=0, keepdims=True))
            s2.append(jnp.max(jnp.where(hit, ci2, -1.0), axis=0, keepdims=True))
            yield
        ev = jnp.exp(jnp.concatenate(sv, axis=0) - sv[0])
        rows = pl.ds(pl.multiple_of(e * PEER_TOPK, PEER_TOPK), PEER_TOPK)
        i1_s[rows, tok] = jnp.concatenate(s1, axis=0)
        i2_s[rows, tok] = jnp.concatenate(s2, axis=0)
        gt_s[rows, tok] = ev * (0.5 / jnp.sum(ev, axis=0, keepdims=True))

    def build_gates(gs_write):
        sub = lax.broadcasted_iota(jnp.int32, (PEER_NKEYS, nslot), 0).astype(F32)
        per_step = tn // PEER_HEADS
        for j in range(per_step):
            n = e * per_step + j
            at = jnp.where(sub == i1t_s[pl.ds(n, 1), :], gtt_s[pl.ds(n, 1), :], 0.0).astype(BF)
            bt = jnp.where(sub == i2t_s[pl.ds(n, 1), :], 1.0, 0.0).astype(BF)
            g = pltpu.bitcast(_nt_dot(at, bt).astype(BF), jnp.uint32)
            gs_write[pl.ds(pl.multiple_of(n * GS_PITCH, 8), _words(PEER_NKEYS)), :] = g
            yield

    def output_block(cols):
        cur = e % 2
        acc_s[:, cols] += _dot(w_s[1 - cur], pltpu.bitcast(v_ref[:, cols], BF))

    def step(gs_read, gs_write, routing, dense):
        routers = [route(slice(k * 128, (k + 1) * 128)) for k in range(tn // 128)] if routing else []
        gater = build_gates(gs_write)
        n_chunk = PEER_TE // PEER_CW
        pops_per_chunk = 3 * PEER_TOPK // n_chunk
        gates_per_chunk = tn // PEER_HEADS // n_chunk
        per_chunk = PEER_CW // PEER_NKEYS
        chunks_per_out = n_chunk * PEER_CW // D
        cur = e % 2
        xn = pltpu.bitcast(xn_ref[...], BF)
        for router in routers:
            next(router)
        for c in range(n_chunk):
            if dense and c % chunks_per_out == 0:
                output_block(slice(c // chunks_per_out * PEER_CW, (c // chunks_per_out + 1) * PEER_CW))
            if dense:
                a = _dot(xn, pltpu.bitcast(ut_ref[:, c * PEER_CW:(c + 1) * PEER_CW], BF))
            for _ in range(pops_per_chunk):
                for router in routers:
                    next(router, None)
            for _ in range(gates_per_chunk):
                next(gater, None)
            if not dense:
                continue
            ws = []
            for k in range(0, per_chunk, 2):
                pair = (e * (PEER_TE // PEER_NKEYS) + c * per_chunk + k) // 2
                words = gs_read[pl.ds(pair, tn, stride=GS_PITCH), :]
                for half, g in enumerate((pltpu.bitcast(words << 16, F32),
                                          pltpu.bitcast(words & jnp.uint32(0xFFFF0000), F32))):
                    ak = a[:, (k + half) * PEER_NKEYS:(k + half + 1) * PEER_NKEYS]
                    ws.append((g * (ak * (1.0 + lax.erf(ak * (0.5 ** 0.5))))).astype(BF))
            w_s[cur, :, c * PEER_CW:(c + 1) * PEER_CW] = jnp.concatenate(ws, axis=1)
        for gen in routers + [gater]:
            for _ in gen:
                pass

    fill, drain = r < 2, (r >= ntiles) & (r < ntiles + 2)
    for parity, bufs in enumerate(((gsa_s, gsb_s), (gsb_s, gsa_s))):
        mine = r % 2 == parity
        pl.when(mine & fill)(functools.partial(step, *bufs, routing=True, dense=False))
        pl.when(mine & (r >= 2) & (r < ntiles))(functools.partial(step, *bufs, routing=True, dense=True))
        pl.when(mine & drain)(functools.partial(step, *bufs, routing=False, dense=True))

    @pl.when((r == ntiles + 2) & (e == 0))
    def _last():
        for c in range(D // PEER_CW):
            output_block(slice(c * PEER_CW, (c + 1) * PEER_CW))

    @pl.when(e == 0)
    def _fin():
        gate = mod_ref[0, 0][gate_idx:gate_idx + 1]
        out_ref[...] = h_ref[...] + gate * acc_s[...]
        acc_s[...] = jnp.zeros_like(acc_s)


def _peer_residual(h2, q2, xn2, keys, ut, v, layer, modtab, *, gate_idx, tok_per_batch, ctx_rows):
    ntok = h2.shape[0]
    nexp = ut.shape[2]
    nslot = PEER_HEADS * PEER_TOPK
    tn, te = PEER_TN, PEER_TE
    ntiles = ntok // tn
    tiles_per_batch = tok_per_batch // tn
    assert nexp // te == PEER_HEADS
    assert ntiles >= 2

    nsteps = nexp // te

    def lagged(r, lag):
        return jnp.clip(r - lag, 0, ntiles - 1)

    def mod_map(r, e):
        t = lagged(r, 3)
        return (t // tiles_per_batch, _is_latent_tile(t % tiles_per_batch, ctx_rows // tn), 0, 0)

    return pl.pallas_call(
        functools.partial(_peer_kernel, gate_idx=gate_idx, ntiles=ntiles),
        grid=(ntiles + 3, nsteps),
        in_specs=[pl.BlockSpec((2, tn, PEER_DHALF), lambda r, e: (e, lagged(r, 0), 0)),
                  pl.BlockSpec((_words(tn), D), lambda r, e: (lagged(r, 2), 0)),
                  pl.BlockSpec((1, 2, PEER_NKEYS, PEER_DHALF), lambda r, e: (e, 0, 0, 0)),
                  pl.BlockSpec((None, _words(D), te), lambda r, e: (layer, 0, e)),
                  pl.BlockSpec((None, _words(te), D), lambda r, e: (layer, (e + nsteps - 1) % nsteps, 0)),
                  pl.BlockSpec((tn, D), lambda r, e: (lagged(r, 3), 0)),
                  pl.BlockSpec((1, 1, N_MOD, D), mod_map)],
        out_specs=pl.BlockSpec((tn, D), lambda r, e: (lagged(r, 3), 0)),
        out_shape=jax.ShapeDtypeStruct((ntok, D), F32),
        scratch_shapes=[pltpu.VMEM((nslot, tn), F32),
                        pltpu.VMEM((nslot, tn), F32),
                        pltpu.VMEM((nslot, tn), F32),
                        pltpu.VMEM((tn, nslot), F32),
                        pltpu.VMEM((tn, nslot), F32),
                        pltpu.VMEM((tn, nslot), F32),
                        pltpu.VMEM((tn * GS_PITCH, PEER_NKEYS), jnp.uint32),
                        pltpu.VMEM((tn * GS_PITCH, PEER_NKEYS), jnp.uint32),
                        pltpu.VMEM((2, tn, te), BF),
                        pltpu.VMEM((tn, D), F32)],
        compiler_params=_cparams(("arbitrary", "arbitrary")),
    )(q2, xn2, keys, ut, v, h2, modtab)


def _pack_kernel(w_ref, o_ref, *, transpose):
    w = w_ref[0].T if transpose else w_ref[0]
    o_ref[0] = pltpu.bitcast(w.astype(BF), jnp.uint32)


def _pack_experts(w, *, transpose):
    nl, ne, d = w.shape
    te = 512
    if transpose:
        out_block, out_map, out_dims = (1, _words(d), te), (lambda l, i: (l, 0, i)), (nl, _words(d), ne)
    else:
        out_block, out_map, out_dims = (1, _words(te), d), (lambda l, i: (l, i, 0)), (nl, _words(ne), d)
    return pl.pallas_call(
        functools.partial(_pack_kernel, transpose=transpose),
        grid=(nl, ne // te),
        in_specs=[pl.BlockSpec((1, te, d), lambda l, i: (l, i, 0))],
        out_specs=pl.BlockSpec(out_block, out_map),
        out_shape=jax.ShapeDtypeStruct(out_dims, jnp.uint32),
        compiler_params=_cparams(("parallel", "parallel")),
    )(w)


def _final_kernel(h_ref, g_ref, o_ref):
    x = h_ref[0]
    o_ref[0] = x * lax.rsqrt(jnp.mean(x * x, axis=-1, keepdims=True) + EPS) * g_ref[...]


def _final_norm(h, g):
    bsz, s, _ = h.shape
    return pl.pallas_call(
        _final_kernel,
        grid=(bsz, s // TM),
        in_specs=[pl.BlockSpec((1, TM, D), lambda b, i: (b, i, 0)),
                  pl.BlockSpec((1, D), lambda b, i: (0, 0))],
        out_specs=pl.BlockSpec((1, TM, D), lambda b, i: (b, i, 0)),
        out_shape=jax.ShapeDtypeStruct((bsz, s, D), F32),
        compiler_params=_cparams(("parallel", "parallel")),
    )(h, g.reshape(1, D))


def _rot_half_cols(w, dh):
    d, n = w.shape
    wh = w.reshape(d, n // dh, 2, dh // 2)
    return jnp.concatenate([-wh[:, :, 1], wh[:, :, 0]], axis=2).reshape(d, n)


def _rope_tables(s, dh, width):
    rows = s // GRID_W
    r = jnp.repeat(jnp.arange(rows, dtype=F32), GRID_W)
    col = jnp.broadcast_to(jnp.arange(GRID_W, dtype=F32), (rows, GRID_W)).reshape(-1)
    nf = dh // 4
    inv = ROPE_THETA ** (-jnp.arange(nf, dtype=F32) / nf)
    ang = jnp.concatenate([r[:, None] * inv, col[:, None] * inv], axis=-1)
    cos = jnp.concatenate([jnp.ones((CTX, dh // 2), F32), jnp.cos(ang)], axis=0)
    sin = jnp.concatenate([jnp.zeros((CTX, dh // 2), F32), jnp.sin(ang)], axis=0)
    reps = width // (dh // 2)
    return jnp.tile(cos, (1, reps)), jnp.tile(sin, (1, reps))


def _interleave_heads(a, nh, dh):
    lead = a.shape[:-1]
    return a.reshape(lead + (2, nh, dh)).swapaxes(-3, -2).reshape(lead + (2 * nh * dh,))


def kernel(x, c, ctx, c_ctx, mod_w, mod_b, norm1_g, norm2_g, ev_w_in, ev_ml_conv_w, ev_ml_conv_b,
           ev_ml_gate_b, ev_ml_norm_g, ev_da_lam, ev_da_norm_g, ev_w_out, od_w_in, od_sink, od_w_out,
           pr_w_q, pr_keys, pr_u, pr_v, final_g):
    bsz, s, _ = x.shape
    depth = mod_w.shape[0]
    t = CTX + s
    ml_w = ML_HEADS * ML_DV
    da_w = DA_HEADS * DA_DV
    qk_w = 2 * ML_HEADS * ML_DQK
    da_qw = DA_HEADS * 2 * DA_DQK

    h = jnp.concatenate([ctx, x], axis=1)

    rows = -(-(bsz + 1) // 8) * 8
    cc = jnp.zeros((rows, D), F32).at[:bsz].set(c).at[bsz].set(c_ctx)
    mods = _modulation(cc, mod_w, mod_b)
    mod_lat = mods[:, :bsz].reshape(depth, bsz, N_MOD, D)
    mod_ctx = jnp.broadcast_to(mods[:, bsz].reshape(depth, 1, N_MOD, D), (depth, bsz, N_MOD, D))
    modtabs = jnp.stack([mod_ctx, mod_lat], axis=2)

    wa_rope_w = (WA_HEADS + WA_KV_HEADS) * WA_DH
    cos_t, sin_t = _rope_tables(s, WA_DH, wa_rope_w)
    ut_words = _pack_experts(pr_u, transpose=True)
    v_words = _pack_experts(pr_v, transpose=False)

    for i in range(depth):
        j = i // 2
        modtab = modtabs[i]
        if i % 2 == 0:
            lam_init = 0.8 - 0.6 * math.exp(-0.3 * i)
            w = ev_w_in[j]
            o0 = 0
            w_mqk = w[:, o0:o0 + qk_w]; o0 += qk_w
            w_mv = w[:, o0:o0 + ml_w]; o0 += ml_w
            w_mo = w[:, o0:o0 + ml_w]; o0 += ml_w
            w_g = w[:, o0:o0 + 4 * ML_HEADS]; o0 += 4 * ML_HEADS
            w_dq = w[:, o0:o0 + da_qw]; o0 += da_qw
            w_dk = w[:, o0:o0 + da_qw]; o0 += da_qw
            w_dv = w[:, o0:o0 + da_w]
            w_rope = jnp.concatenate([w_dq, w_dk], axis=1)
            gpad = jnp.zeros((D, 128 - 4 * ML_HEADS), F32)
            wm = jnp.concatenate([w_rope, _interleave_heads(w_mqk, ML_HEADS, ML_DQK), w_mv, w_mo, w_dv, w_g, gpad],
                                 axis=1).astype(BF)
            wr = _rot_half_cols(w_rope, DA_DQK).astype(BF)
            nr = 2 * da_qw
            col_mqk = nr
            col_mv = col_mqk + qk_w
            col_mo = col_mv + ml_w
            col_dv = col_mo + ml_w
            col_g = col_dv + da_w
            (y,) = _norm_mod_proj(h, modtab, norm1_g[i], wm, wr, cos_t[:, :nr], sin_t[:, :nr],
                                  shift_idx=0, scale_idx=1)
            gates_t = jnp.swapaxes(y[:, :, col_g:col_g + 4 * ML_HEADS], 1, 2)
            gk, qs = _gate_scan(gates_t, ev_ml_gate_b[j])
            qs = qs.reshape(bsz, 4, ML_HEADS, t).transpose(0, 2, 3, 1)
            ml = _mlstm(y, gk, qs, _interleave_heads(ev_ml_conv_w[j], ML_HEADS, ML_DQK),
                        _interleave_heads(ev_ml_conv_b[j], ML_HEADS, ML_DQK), ev_ml_norm_g[j],
                        col_qk=col_mqk, col_v=col_mv, col_o=col_mo)
            da = _diff_attention(y, ev_da_lam[j], ev_da_norm_g[j], col_q=0, col_k=da_qw, col_v=col_dv,
                                 lam_init=lam_init)
            w_out = ev_w_out[j].astype(BF)
            h = _outproj_residual(h, modtab, [(ml, w_out[:ml_w]), (da, w_out[ml_w:])], gate_idx=2)
        else:
            w = od_w_in[j]
            wm = w.astype(BF)
            wr = _rot_half_cols(w[:, :wa_rope_w], WA_DH).astype(BF)
            (y,) = _norm_mod_proj(h, modtab, norm1_g[i], wm, wr, cos_t, sin_t, shift_idx=0, scale_idx=1)
            att = _window_attention(y, od_sink[j], col_q=0, col_k=WA_HEADS * WA_DH,
                                    col_v=WA_HEADS * WA_DH + WA_KV_HEADS * WA_DH)
            h = _outproj_residual(h, modtab, [(att, od_w_out[j].astype(BF))], gate_idx=2)
        ctx_rows = CTX if i < depth - 1 else 0
        hp = h if ctx_rows else h[:, CTX:]
        tp = hp.shape[1]
        q, xn = _norm_mod_proj(hp, modtab, norm2_g[i], pr_w_q[i].astype(BF), shift_idx=3, scale_idx=4,
                               want_xn=True, split=PEER_DHALF, ctx_rows=ctx_rows, tm=_row_tile(tp))
        h = _peer_residual(hp.reshape(bsz * tp, D), q, xn.reshape(-1, D), pr_keys[i].astype(BF),
                           ut_words, v_words, i, modtab,
                           gate_idx=5, tok_per_batch=tp, ctx_rows=ctx_rows).reshape(bsz, tp, D)
    return _final_norm(h, final_g)
```
